```python
import jax, jax.numpy as jnp
from jax import lax
import numpy as np

D_MODEL = 1024
BATCH = 4
SEQ = 4096
DEPTH = 2

PLE_DIM = 256
D_FF = 2816
LN_EPS = 1e-5
RMS_EPS = 1e-6
DEEPNORM_ALPHA = (2 * DEPTH) ** 0.25
DEEPNORM_BETA = (8 * DEPTH) ** -0.25

GDN_HEADS = 8
GDN_DK = 64
GDN_DV = 64
GDN_CONV = 4
GDN_CHUNK = 64

SB_HEADS = 4
SB_DIM = 64

MLA_HEADS = 4
MLA_NOPE = 64
MLA_ROPE = 32
MLA_V = 64
MLA_Q_RANK = 256
MLA_KV_RANK = 128
ROPE_BASE = 10000.0

Q_BLOCK = 128

MIX_WIDTH = GDN_HEADS * GDN_DV + SB_HEADS * SB_DIM + MLA_HEADS * MLA_V
IN_WIDTHS = (
    GDN_HEADS * GDN_DK, GDN_HEADS * GDN_DK, GDN_HEADS * GDN_DV,
    GDN_HEADS * GDN_DV,
    GDN_HEADS, GDN_HEADS,
    SB_HEADS * SB_DIM, SB_HEADS * SB_DIM, SB_HEADS * SB_DIM,
    MLA_Q_RANK,
    MLA_KV_RANK + MLA_ROPE,
)
IN_TOTAL = int(sum(IN_WIDTHS))
IN_SPLITS = tuple(int(s) for s in np.cumsum(IN_WIDTHS)[:-1])
GDN_CONV_CH = 2 * GDN_HEADS * GDN_DK + GDN_HEADS * GDN_DV

kernel_name = "hybrid_gdn_stickbreak_mla_macaron_deepnorm"


def layer_norm(x, g, b):
    xf = x.astype(jnp.float32)
    mu = jnp.mean(xf, axis=-1, keepdims=True)
    var = jnp.mean(jnp.square(xf - mu), axis=-1, keepdims=True)
    y = (xf - mu) * lax.rsqrt(var + LN_EPS)
    return (y * g.astype(jnp.float32) + b.astype(jnp.float32)).astype(x.dtype)


def rms_norm(x, w):
    xf = x.astype(jnp.float32)
    y = xf * lax.rsqrt(jnp.mean(jnp.square(xf), axis=-1, keepdims=True) + RMS_EPS)
    return (y * w.astype(jnp.float32)).astype(x.dtype)


def l2_normalize(x):
    xf = x.astype(jnp.float32)
    return xf * lax.rsqrt(jnp.sum(jnp.square(xf), axis=-1, keepdims=True) + RMS_EPS)


def swiglu(h, w_in, w_out):
    gate, up = jnp.split(h @ w_in, 2, axis=-1)
    return (jax.nn.silu(gate) * up) @ w_out


def causal_depthwise_conv(x, w):
    k_width, ch = w.shape
    return lax.conv_general_dilated(
        x, w[:, None, :].astype(x.dtype), window_strides=(1,), padding=[(k_width - 1, 0)],
        dimension_numbers=("NWC", "WIO", "NWC"), feature_group_count=ch)


def rope_tables(positions):
    inv = 1.0 / (ROPE_BASE ** (jnp.arange(0, MLA_ROPE, 2, dtype=jnp.float32) / MLA_ROPE))
    ang = positions.astype(jnp.float32)[..., None] * inv
    return jnp.cos(ang), jnp.sin(ang)


def apply_rope(x, cos, sin):
    x1, x2 = jnp.split(x.astype(jnp.float32), 2, axis=-1)
    return jnp.concatenate([x1 * cos - x2 * sin, x2 * cos + x1 * sin], axis=-1).astype(x.dtype)


def chunk_gated_delta_rule(q, k, v, g, beta):
    B, S, H, DK = q.shape
    DV = v.shape[-1]
    C = GDN_CHUNK
    N = S // C
    f32 = jnp.float32

    def chunks(t):
        t = t.astype(f32).reshape((B, N, C) + t.shape[2:])
        return jnp.moveaxis(jnp.moveaxis(t, 3, 2), 1, 0)

    qc, kc, vc, gc, bc = (chunks(t) for t in (q, k, v, g, beta))
    gc = jnp.cumsum(gc, axis=-1)
    incl = jnp.tril(jnp.ones((C, C), dtype=bool))
    strict = jnp.tril(jnp.ones((C, C), dtype=bool), -1)
    diff = gc[..., :, None] - gc[..., None, :]
    decay = jnp.where(incl, jnp.exp(jnp.where(incl, diff, 0.0)), 0.0)
    kb = kc * bc[..., None]
    lhs = jnp.where(strict, jnp.einsum('nbhid,nbhjd->nbhij', kb, kc) * decay, 0.0) + jnp.eye(C, dtype=f32)
    u = lax.linalg.triangular_solve(lhs, vc * bc[..., None], left_side=True, lower=True, unit_diagonal=True)
    w = lax.linalg.triangular_solve(lhs, kb * jnp.exp(gc)[..., None], left_side=True, lower=True, unit_diagonal=True)
    qk = jnp.where(incl, jnp.einsum('nbhid,nbhjd->nbhij', qc, kc) * decay, 0.0)
    q_dec = qc * jnp.exp(gc)[..., None]
    k_dec = kc * jnp.exp(gc[..., -1:] - gc)[..., None]
    g_last = jnp.exp(gc[..., -1])

    def step(state, xs):
        u_n, w_n, qk_n, qd_n, kd_n, gl_n = xs
        v_new = u_n - jnp.einsum('bhck,bhkv->bhcv', w_n, state)
        o_n = jnp.einsum('bhck,bhkv->bhcv', qd_n, state) + jnp.einsum('bhij,bhjv->bhiv', qk_n, v_new)
        state = state * gl_n[..., None, None] + jnp.einsum('bhck,bhcv->bhkv', kd_n, v_new)
        return state, o_n

    s0 = jnp.zeros((B, H, DK, DV), f32)
    _, o = lax.scan(step, s0, (u, w, qk, q_dec, k_dec, g_last))
    return jnp.transpose(o, (1, 0, 3, 2, 4)).reshape(B, S, H, DV)


def gated_deltanet(gq, gk, gv, gz, ga, gb, conv_w, a_log, dt_bias, norm_w):
    B, S, _ = gq.shape
    f32 = jnp.float32
    qkv = jax.nn.silu(causal_depthwise_conv(jnp.concatenate([gq, gk, gv], axis=-1), conv_w))
    q, k, v = jnp.split(qkv, [GDN_HEADS * GDN_DK, 2 * GDN_HEADS * GDN_DK], axis=-1)
    q = l2_normalize(q.reshape(B, S, GDN_HEADS, GDN_DK)) * (GDN_DK ** -0.5)
    k = l2_normalize(k.reshape(B, S, GDN_HEADS, GDN_DK))
    v = v.reshape(B, S, GDN_HEADS, GDN_DV)
    beta = jax.nn.sigmoid(gb.astype(f32))
    g = -jnp.exp(a_log.astype(f32)) * jax.nn.softplus(ga.astype(f32) + dt_bias.astype(f32))
    o = chunk_gated_delta_rule(q, k, v, g, beta)
    o = rms_norm(o, norm_w) * jax.nn.silu(gz.reshape(B, S, GDN_HEADS, GDN_DV).astype(f32))
    return o.reshape(B, S, GDN_HEADS * GDN_DV).astype(gq.dtype)


def to_query_blocks(t):
    B, S = t.shape[:2]
    return jnp.swapaxes(t.reshape((B, S // Q_BLOCK, Q_BLOCK) + t.shape[2:]), 0, 1)


def from_query_blocks(t):
    t = jnp.swapaxes(t, 0, 1)
    return t.reshape((t.shape[0], t.shape[1] * t.shape[2]) + t.shape[3:])


def stick_breaking_attention(q, k, v):
    S = q.shape[1]
    scale = SB_DIM ** -0.5
    kpos = jnp.arange(S)

    def block(args):
        qb, i = args
        z = jnp.einsum('bqhd,bkhd->bhqk', qb, k).astype(jnp.float32) * scale
        qpos = i * Q_BLOCK + jnp.arange(Q_BLOCK)
        mask = kpos[None, :] < qpos[:, None]
        log_1m = jnp.where(mask, jax.nn.log_sigmoid(-z), 0.0)
        rest = lax.cumsum(log_1m, axis=3, reverse=True) - log_1m
        wts = jnp.where(mask, jnp.exp(jax.nn.log_sigmoid(z) + rest), 0.0)
        return jnp.einsum('bhqk,bkhd->bqhd', wts.astype(v.dtype), v)

    nb = S // Q_BLOCK
    out = lax.map(block, (to_query_blocks(q), jnp.arange(nb)))
    return from_query_blocks(out)


def mla_attention(q_nope, q_rope, k_nope, k_rope, v):
    S = q_nope.shape[1]
    scale = (MLA_NOPE + MLA_ROPE) ** -0.5
    kpos = jnp.arange(S)

    def block(args):
        qn, qr, i = args
        s = (jnp.einsum('bqhd,bkhd->bhqk', qn, k_nope) + jnp.einsum('bqhd,bkd->bhqk', qr, k_rope)).astype(jnp.float32) * scale
        qpos = i * Q_BLOCK + jnp.arange(Q_BLOCK)
        s = jnp.where(kpos[None, :] <= qpos[:, None], s, -jnp.inf)
        pr = jax.nn.softmax(s, axis=-1)
        return jnp.einsum('bhqk,bkhd->bqhd', pr.astype(v.dtype), v)

    nb = S // Q_BLOCK
    out = lax.map(block, (to_query_blocks(q_nope), to_query_blocks(q_rope), jnp.arange(nb)))
    return from_query_blocks(out)


def hybrid_mixer(h, cos, sin, w_in, conv_w, a_log, dt_bias, gdn_norm_w, q_norm_w, kv_norm_w, w_uq, w_ukv, w_o):
    B, S, _ = h.shape
    proj = h @ w_in
    gq, gk, gv, gz, ga, gb, sq, sk, sv, mq, mkv = jnp.split(proj, IN_SPLITS, axis=-1)
    o_gdn = gated_deltanet(gq, gk, gv, gz, ga, gb, conv_w, a_log, dt_bias, gdn_norm_w)
    shp = (B, S, SB_HEADS, SB_DIM)
    o_sb = stick_breaking_attention(sq.reshape(shp), sk.reshape(shp), sv.reshape(shp)).reshape(B, S, SB_HEADS * SB_DIM)
    qf = (rms_norm(mq, q_norm_w) @ w_uq).reshape(B, S, MLA_HEADS, MLA_NOPE + MLA_ROPE)
    q_nope, q_rope = jnp.split(qf, [MLA_NOPE], axis=-1)
    ckv, k_rope = jnp.split(mkv, [MLA_KV_RANK], axis=-1)
    kvf = (rms_norm(ckv, kv_norm_w) @ w_ukv).reshape(B, S, MLA_HEADS, MLA_NOPE + MLA_V)
    k_nope, v_mla = jnp.split(kvf, [MLA_NOPE], axis=-1)
    q_rope = apply_rope(q_rope, cos[:, :, None, :], sin[:, :, None, :])
    k_rope = apply_rope(k_rope, cos, sin)
    o_mla = mla_attention(q_nope, q_rope, k_nope, k_rope, v_mla).reshape(B, S, MLA_HEADS * MLA_V)
    return jnp.concatenate([o_gdn, o_sb, o_mla], axis=-1) @ w_o


def setup_inputs(seed: int = 0) -> dict:
    key = jax.random.key(seed)
    ks = jax.random.split(key, 24)
    f32 = jnp.float32
    L = DEPTH

    def nrm(k, shape, scale):
        return jax.random.normal(k, shape, f32) * scale

    x = nrm(ks[0], (BATCH, SEQ, D_MODEL), 1.0)
    p = nrm(ks[1], (DEPTH, BATCH, SEQ, PLE_DIM), 1.0)
    positions = (jnp.arange(SEQ, dtype=jnp.int32)[None, :]
                 + jax.random.randint(ks[2], (BATCH, 1), 0, 1024, dtype=jnp.int32))
    a_init = jax.random.uniform(ks[7], (L, GDN_HEADS), f32, 1.0, 16.0)
    dt = jnp.exp(jax.random.uniform(ks[8], (L, GDN_HEADS), f32, np.log(1e-3), np.log(1e-1)))
    return {
        "x": x,
        "p": p,
        "positions": positions,
        "ffa_w_in": nrm(ks[3], (L, D_MODEL, 2 * D_FF), D_MODEL ** -0.5),
        "ffa_w_out": nrm(ks[4], (L, D_FF, D_MODEL), D_FF ** -0.5 * DEEPNORM_BETA),
        "mix_w_in": nrm(ks[5], (L, D_MODEL, IN_TOTAL), D_MODEL ** -0.5),
        "gdn_conv_w": nrm(ks[6], (L, GDN_CONV, GDN_CONV_CH), GDN_CONV ** -0.5),
        "gdn_a_log": jnp.log(a_init),
        "gdn_dt_bias": dt + jnp.log(-jnp.expm1(-dt)),
        "gdn_norm_w": 1.0 + nrm(ks[9], (L, GDN_DV), 0.02),
        "mla_q_norm_w": 1.0 + nrm(ks[10], (L, MLA_Q_RANK), 0.02),
        "mla_kv_norm_w": 1.0 + nrm(ks[11], (L, MLA_KV_RANK), 0.02),
        "mla_w_uq": nrm(ks[12], (L, MLA_Q_RANK, MLA_HEADS * (MLA_NOPE + MLA_ROPE)), MLA_Q_RANK ** -0.5),
        "mla_w_ukv": nrm(ks[13], (L, MLA_KV_RANK, MLA_HEADS * (MLA_NOPE + MLA_V)), MLA_KV_RANK ** -0.5),
        "mix_w_o": nrm(ks[14], (L, MIX_WIDTH, D_MODEL), MIX_WIDTH ** -0.5 * DEEPNORM_BETA),
        "ffb_w_in": nrm(ks[15], (L, D_MODEL, 2 * D_FF), D_MODEL ** -0.5),
        "ffb_w_out": nrm(ks[16], (L, D_FF, D_MODEL), D_FF ** -0.5 * DEEPNORM_BETA),
        "ln_g": 1.0 + nrm(ks[17], (L, 3, D_MODEL), 0.02),
        "ln_b": nrm(ks[18], (L, 3, D_MODEL), 0.02),
        "ple_w_gate": nrm(ks[19], (L, D_MODEL, D_MODEL), D_MODEL ** -0.5),
        "ple_w_proj": nrm(ks[20], (L, PLE_DIM, D_MODEL), PLE_DIM ** -0.5 * DEEPNORM_BETA),
    }


def reference(x, p, positions, ffa_w_in, ffa_w_out, mix_w_in, gdn_conv_w, gdn_a_log, gdn_dt_bias,
              gdn_norm_w, mla_q_norm_w, mla_kv_norm_w, mla_w_uq, mla_w_ukv, mix_w_o,
              ffb_w_in, ffb_w_out, ln_g, ln_b, ple_w_gate, ple_w_proj):
    cos, sin = rope_tables(positions)
    h = x
    for i in range(DEPTH):
        h = layer_norm(DEEPNORM_ALPHA * h + 0.5 * swiglu(h, ffa_w_in[i], ffa_w_out[i]), ln_g[i, 0], ln_b[i, 0])
        mix = hybrid_mixer(h, cos, sin, mix_w_in[i], gdn_conv_w[i], gdn_a_log[i], gdn_dt_bias[i], gdn_norm_w[i],
                           mla_q_norm_w[i], mla_kv_norm_w[i], mla_w_uq[i], mla_w_ukv[i], mix_w_o[i])
        h = layer_norm(DEEPNORM_ALPHA * h + mix, ln_g[i, 1], ln_b[i, 1])
        h = layer_norm(DEEPNORM_ALPHA * h + 0.5 * swiglu(h, ffb_w_in[i], ffb_w_out[i]), ln_g[i, 2], ln_b[i, 2])
        h = h + jax.nn.sigmoid(h @ ple_w_gate[i]) * (p[i] @ ple_w_proj[i])
    return h
```

```python
import functools

import numpy as np
import jax
import jax.numpy as jnp
from jax import lax
from jax.experimental import pallas as pl
from jax.experimental.pallas import tpu as pltpu

F32 = jnp.float32
BF16 = jnp.bfloat16

DEPTH = 2
D_MODEL = 1024
PLE_DIM = 256
D_FF = 2816
LN_EPS = 1e-5
RMS_EPS = 1e-6
DEEPNORM_ALPHA = (2 * DEPTH) ** 0.25

GDN_HEADS = 8
GDN_DK = 64
GDN_DV = 64
GDN_CONV = 4
SB_HEADS = 4
SB_DIM = 64
MLA_HEADS = 4
MLA_NOPE = 64
MLA_ROPE = 32
MLA_V = 64
MLA_Q_RANK = 256
MLA_KV_RANK = 128
ROPE_BASE = 10000.0

LANES = 128
HALF = LANES // 2
VMEM_LIMIT = 56 * 1024 * 1024

GDN_CHUNK = 128
TOKEN_TILE = 256
ATTN_BLOCK = 128

N_QK = GDN_HEADS * LANES
N_VZ = GDN_HEADS * LANES
N_SB = 3 * SB_HEADS * SB_DIM
N_ML = MLA_Q_RANK + MLA_KV_RANK + 2 * LANES
ML_M1 = (MLA_Q_RANK + MLA_KV_RANK) // LANES
A_LANE = 0
B_LANE = GDN_HEADS
N_MIX = N_QK + N_VZ + N_SB + N_ML


def _silu(x):
    return x / (1.0 + jnp.exp(-x))


def _sigmoid(x):
    return 1.0 / (1.0 + jnp.exp(-x))


def _softplus(x):
    return jnp.maximum(x, 0.0) + jnp.log1p(jnp.exp(-jnp.abs(x)))


def _layer_norm(r, g, b):
    mu = jnp.mean(r, axis=-1, keepdims=True)
    d = r - mu
    var = jnp.mean(d * d, axis=-1, keepdims=True)
    return d * lax.rsqrt(var + LN_EPS) * g + b


def _dot(a, b):
    return jnp.dot(a, b, preferred_element_type=F32)


def _dot_nt(a, b):
    return lax.dot_general(a, b, (((1,), (1,)), ((), ())), preferred_element_type=F32)


def _split2(x):
    hi = x.astype(BF16)
    lo = (x - hi.astype(F32)).astype(BF16)
    return hi, lo


def _split3(x):
    hi = x.astype(BF16)
    r = x - hi.astype(F32)
    mid = r.astype(BF16)
    lo = (r - mid.astype(F32)).astype(BF16)
    return hi, mid, lo


def _swiglu(xb, w_in_ref, w_out_ref):
    gu = _dot(xb, w_in_ref[...])
    act = _silu(gu[:, :D_FF]) * gu[:, D_FF:]
    return _dot(act.astype(BF16), w_out_ref[...])


def _resident(shape):
    return pl.BlockSpec(shape, lambda *_: (0,) * len(shape), pipeline_mode=pl.Buffered(1))


def _ffn_inproj_kernel(h_ref, w_in_ref, w_out_ref, g_ref, b_ref, w_mix_ref,
                       h_out, qk_out, vz_out, sb_out, ml_out):
    x = h_ref[...]
    y = _swiglu(x.astype(BF16), w_in_ref, w_out_ref)
    hn = _layer_norm(DEEPNORM_ALPHA * x + 0.5 * y, g_ref[...], b_ref[...])
    h_out[...] = hn
    proj = _dot(hn.astype(BF16), w_mix_ref[...])
    qk_out[...] = proj[:, :N_QK]
    vz_out[...] = proj[:, N_QK:N_QK + N_VZ]
    sb_out[...] = proj[:, N_QK + N_VZ:N_QK + N_VZ + N_SB].astype(BF16)
    ml_out[...] = proj[:, N_QK + N_VZ + N_SB:]


def _ffn_inproj(h, w_in, w_out, g, b, w_mix):
    t = h.shape[0]
    tm = TOKEN_TILE
    row = lambda n: pl.BlockSpec((tm, n), lambda i: (i, 0))
    return pl.pallas_call(
        _ffn_inproj_kernel,
        grid=(t // tm,),
        in_specs=[row(D_MODEL), _resident(w_in.shape), _resident(w_out.shape),
                  _resident(g.shape), _resident(b.shape), _resident(w_mix.shape)],
        out_specs=[row(D_MODEL), row(N_QK), row(N_VZ), row(N_SB), row(N_ML)],
        out_shape=[jax.ShapeDtypeStruct((t, D_MODEL), F32),
                   jax.ShapeDtypeStruct((t, N_QK), F32),
                   jax.ShapeDtypeStruct((t, N_VZ), F32),
                   jax.ShapeDtypeStruct((t, N_SB), BF16),
                   jax.ShapeDtypeStruct((t, N_ML), F32)],
        compiler_params=pltpu.CompilerParams(dimension_semantics=("arbitrary",),
                                             vmem_limit_bytes=VMEM_LIMIT),
        name="ffn_inproj",
    )(h, w_in, w_out, g, b, w_mix)


def _causal_conv(x_ref, w_ref, buf_ref, first):
    c = x_ref.shape[0]

    @pl.when(first)
    def _():
        buf_ref[0:8, :] = jnp.zeros((8, buf_ref.shape[1]), F32)

    x = x_ref[...]
    buf_ref[8:8 + c, :] = x
    y = x * w_ref[GDN_CONV - 1:GDN_CONV, :]
    for j in range(GDN_CONV - 1):
        off = 8 - (GDN_CONV - 1) + j
        y = y + buf_ref[off:off + c, :] * w_ref[j:j + 1, :]
    buf_ref[0:8, :] = x[c - 8:, :]
    return y


def _unit_lower_inverse(l_strict, row, col):
    c = l_strict.shape[0]
    blk = lambda v, log2: jnp.right_shift(v, log2)
    pair = (blk(row, 1) == blk(col, 1)) & (row > col)
    x = jnp.where(row == col, 1.0, 0.0) - jnp.where(pair, l_strict, 0.0)
    log2 = 1
    while (2 << log2) <= c:
        off = (blk(row, log2 + 1) == blk(col, log2 + 1)) & (blk(row, log2) > blk(col, log2))
        l_off = jnp.where(off, l_strict, 0.0).astype(BF16)
        xb = x.astype(BF16)
        x = x - _dot(xb, _dot(l_off, xb).astype(BF16))
        log2 += 1
    return x


def _gdn_kernel(qk_ref, vz_ref, m1_ref, cw_qk_ref, cw_vz_ref, alog_ref, dtb_ref, nw_ref,
                o_ref, buf_qk, buf_vz, state_ref):
    c = qk_ref.shape[0]
    first = pl.program_id(1) == 0

    @pl.when(first)
    def _():
        state_ref[...] = jnp.zeros(state_ref.shape, F32)

    lane = lax.broadcasted_iota(jnp.int32, (1, LANES), 1)
    lo = lane < HALF
    row = lax.broadcasted_iota(jnp.int32, (c, c), 0)
    col = lax.broadcasted_iota(jnp.int32, (c, c), 1)
    incl = row >= col
    strict = row > col
    tril = jnp.where(incl, 1.0, 0.0).astype(BF16)

    qk = _silu(_causal_conv(qk_ref, cw_qk_ref, buf_qk, first))
    vz_raw = vz_ref[...]
    vconv = _causal_conv(vz_ref, cw_vz_ref, buf_vz, first)
    m1 = m1_ref[...]
    g_all = -jnp.exp(alog_ref[...]) * _softplus(m1 + dtb_ref[...])
    beta_all = _sigmoid(m1)
    g1, g2, g3 = _split3(g_all)
    gc_all = _dot(tril, g1) + _dot(tril, g2) + _dot(tril, g3)
    gc_rows = gc_all.T

    outs = []
    for h in range(GDN_HEADS):
        sl = slice(h * LANES, (h + 1) * LANES)
        x = qk[:, sl]
        sq = x * x
        ss_q = jnp.sum(jnp.where(lo, sq, 0.0), axis=-1, keepdims=True)
        ss_k = jnp.sum(jnp.where(lo, 0.0, sq), axis=-1, keepdims=True)
        xn = x * jnp.where(lo, lax.rsqrt(ss_q + RMS_EPS) * (GDN_DK ** -0.5), lax.rsqrt(ss_k + RMS_EPS))
        q_lo = jnp.where(lo, xn, 0.0)
        k_lo = jnp.where(lo, pltpu.roll(xn, HALF, axis=1), 0.0)
        vz = jnp.where(lo, _silu(vconv[:, sl]), 0.0)

        gc_col = gc_all[:, A_LANE + h:A_LANE + h + 1]
        gc_row = gc_rows[A_LANE + h:A_LANE + h + 1, :]
        gc_last = gc_row[:, c - 1:c]
        beta = beta_all[:, B_LANE + h:B_LANE + h + 1]
        decay = jnp.where(incl, jnp.exp(jnp.where(incl, gc_col - gc_row, 0.0)), 0.0)
        e_col = jnp.exp(gc_col)

        k_b = k_lo.astype(BF16)
        gram = _dot_nt(jnp.concatenate([q_lo, k_lo], axis=0).astype(BF16), k_b)
        qk_m = gram[:c] * decay
        l_strict = jnp.where(strict, gram[c:] * decay * beta, 0.0)
        t_inv = _unit_lower_inverse(l_strict, row, col).astype(BF16)
        u = _dot(t_inv, (vz * beta).astype(BF16))
        w = _dot(t_inv, (k_lo * (beta * e_col)).astype(BF16))

        s_old = state_ref[h]
        s_b = s_old.astype(BF16)
        v_new = u - _dot(w.astype(BF16), s_b)
        v_b = v_new.astype(BF16)
        o = _dot((q_lo * e_col).astype(BF16), s_b) + _dot(qk_m.astype(BF16), v_b)
        k_dec = k_lo * jnp.exp(gc_last - gc_col)
        state_ref[h] = s_old * jnp.exp(gc_last) + _dot(k_dec.T.astype(BF16), v_b)

        ms = jnp.sum(o * o, axis=-1, keepdims=True) * (1.0 / GDN_DV)
        z = pltpu.roll(vz_raw[:, sl], HALF, axis=1)
        outs.append(o * lax.rsqrt(ms + RMS_EPS) * nw_ref[...] * _silu(z))

    for p in range(GDN_HEADS // 2):
        pair = jnp.where(lo, outs[2 * p], pltpu.roll(outs[2 * p + 1], HALF, axis=1))
        o_ref[:, p * LANES:(p + 1) * LANES] = pair.astype(o_ref.dtype)


def _gdn(qk, vz, ml, cw_qk, cw_vz, alog, dtb, nw, batch):
    t = qk.shape[0]
    c = GDN_CHUNK
    nc = t // batch // c
    blk = lambda n, j: pl.BlockSpec((c, n), lambda b, i: (b * nc + i, j))
    small = lambda a: pl.BlockSpec(a.shape, lambda b, i: (0, 0))
    return pl.pallas_call(
        _gdn_kernel,
        grid=(batch, nc),
        in_specs=[blk(N_QK, 0), blk(N_VZ, 0), blk(LANES, ML_M1),
                  small(cw_qk), small(cw_vz), small(alog), small(dtb), small(nw)],
        out_specs=blk(GDN_HEADS * GDN_DV, 0),
        out_shape=jax.ShapeDtypeStruct((t, GDN_HEADS * GDN_DV), BF16),
        scratch_shapes=[pltpu.VMEM((c + 8, N_QK), F32), pltpu.VMEM((c + 8, N_VZ), F32),
                        pltpu.VMEM((GDN_HEADS, LANES, LANES), F32)],
        compiler_params=pltpu.CompilerParams(dimension_semantics=("arbitrary", "arbitrary"),
                                             vmem_limit_bytes=VMEM_LIMIT),
        name="gdn",
    )(qk, vz, ml, cw_qk, cw_vz, alog, dtb, nw)


def _mla_prep_kernel(ml_ref, cs_ref, qnw_ref, kvnw_ref, w_uq_ref, w_ukv_ref, q_out, k_out, v_out):
    nq = MLA_HEADS * LANES
    ml = ml_ref[...]
    mq = ml[:, :MLA_Q_RANK]
    ckv = ml[:, MLA_Q_RANK:MLA_Q_RANK + MLA_KV_RANK]
    m1 = ml[:, ML_M1 * LANES:(ML_M1 + 1) * LANES]
    m2 = ml[:, (ML_M1 + 1) * LANES:]
    cos = cs_ref[:, :LANES]
    sin = cs_ref[:, LANES:]
    lane = lax.broadcasted_iota(jnp.int32, (1, LANES), 1)
    rot = (lane >= MLA_NOPE) & (lane < MLA_NOPE + MLA_ROPE)

    qn = mq * lax.rsqrt(jnp.mean(mq * mq, axis=-1, keepdims=True) + RMS_EPS) * qnw_ref[...]
    qf = _dot(qn.astype(BF16), w_uq_ref[...])
    scale = (MLA_NOPE + MLA_ROPE) ** -0.5
    kvn = ckv * lax.rsqrt(jnp.mean(ckv * ckv, axis=-1, keepdims=True) + RMS_EPS) * kvnw_ref[...]
    kv = _dot(kvn.astype(BF16), w_ukv_ref[...])
    k_rot = jnp.where(rot, m1 * cos, 0.0) + m2 * sin
    for h in range(MLA_HEADS):
        sl = slice(h * LANES, (h + 1) * LANES)
        q_h = qf[:, sl] * cos + qf[:, nq + h * LANES:nq + (h + 1) * LANES] * sin
        q_out[:, sl] = (q_h * scale).astype(BF16)
        k_out[:, sl] = (kv[:, sl] + k_rot).astype(BF16)
    v_out[...] = kv[:, nq:].astype(BF16)


def _mla_prep(ml, cs, qnw, kvnw, w_uq, w_ukv):
    t = ml.shape[0]
    tm = TOKEN_TILE
    row = lambda n: pl.BlockSpec((tm, n), lambda i: (i, 0))
    nq = MLA_HEADS * LANES
    nv = MLA_HEADS * MLA_V
    return pl.pallas_call(
        _mla_prep_kernel,
        grid=(t // tm,),
        in_specs=[row(N_ML), row(2 * LANES), _resident(qnw.shape), _resident(kvnw.shape),
                  _resident(w_uq.shape), _resident(w_ukv.shape)],
        out_specs=[row(nq), row(nq), row(nv)],
        out_shape=[jax.ShapeDtypeStruct((t, nq), BF16), jax.ShapeDtypeStruct((t, nq), BF16),
                   jax.ShapeDtypeStruct((t, nv), BF16)],
        compiler_params=pltpu.CompilerParams(dimension_semantics=("arbitrary",),
                                             vmem_limit_bytes=VMEM_LIMIT),
        name="mla_prep",
    )(ml, cs, qnw, kvnw, w_uq, w_ukv)


def _sb_kernel(q_ref, k_ref, v_ref, o_ref):
    bq = q_ref.shape[0]
    bk = bq
    i = pl.program_id(2)
    lane = lax.broadcasted_iota(jnp.int32, (1, LANES), 1)
    lo = lane < HALF
    q = q_ref[...] * jnp.asarray(SB_DIM ** -0.5, BF16)
    zero = jnp.zeros_like(q)
    q_heads = (jnp.where(lo, q, zero), jnp.where(lo, zero, q))
    row = lax.broadcasted_iota(jnp.int32, (bq, bk), 0)
    col = lax.broadcasted_iota(jnp.int32, (bq, bk), 1)
    later = jnp.where(row > col, 1.0, 0.0).astype(BF16)
    causal = col < row

    def visit(j, carry, masked):
        start = pl.multiple_of(j * bk, bk)
        k_b = k_ref[pl.ds(start, bk), :]
        v_b = v_ref[pl.ds(start, bk), :]
        new = []
        for hh in range(2):
            run, acc = carry[hh]
            z = _dot_nt(q_heads[hh], k_b)
            sp = _softplus(z)
            log_1m = -sp
            if masked:
                log_1m = jnp.where(causal, log_1m, 0.0)
            hi, low = _split2(log_1m)
            rest = run + _dot(hi, later) + _dot(low, later)
            wts = jnp.exp(z - sp + rest)
            if masked:
                wts = jnp.where(causal, wts, 0.0)
            acc = acc + _dot(wts.astype(BF16), v_b)
            run = run + jnp.sum(log_1m, axis=-1, keepdims=True)
            new.append((run, acc))
        return tuple(new)

    init = tuple((jnp.zeros((bq, 1), F32), jnp.zeros((bq, LANES), F32)) for _ in range(2))
    carry = visit(i, init, True)
    carry = lax.fori_loop(0, i, lambda n, cr: visit(i - 1 - n, cr, False), carry)
    o_ref[...] = jnp.where(lo, carry[0][1], carry[1][1]).astype(o_ref.dtype)


def _sb_attn(sb, batch):
    t = sb.shape[0]
    s = t // batch
    bq = ATTN_BLOCK
    nq = s // bq
    pairs = SB_HEADS // 2
    return pl.pallas_call(
        _sb_kernel,
        grid=(batch, pairs, nq),
        in_specs=[pl.BlockSpec((bq, LANES), lambda b, p, i: (b * nq + i, p)),
                  pl.BlockSpec((s, LANES), lambda b, p, i: (b, pairs + p)),
                  pl.BlockSpec((s, LANES), lambda b, p, i: (b, 2 * pairs + p))],
        out_specs=pl.BlockSpec((bq, LANES), lambda b, p, i: (b * nq + i, p)),
        out_shape=jax.ShapeDtypeStruct((t, SB_HEADS * SB_DIM), BF16),
        compiler_params=pltpu.CompilerParams(dimension_semantics=("arbitrary",) * 3,
                                             vmem_limit_bytes=VMEM_LIMIT),
        name="sb_attn",
    )(sb, sb, sb)


def _mla_kernel(q_ref, k_ref, v_ref, o_ref):
    bq = q_ref.shape[0]
    bk = bq
    i = pl.program_id(2)
    lane = lax.broadcasted_iota(jnp.int32, (1, LANES), 1)
    lo = lane < HALF
    q = q_ref[...]
    row = lax.broadcasted_iota(jnp.int32, (bq, bk), 0)
    col = lax.broadcasted_iota(jnp.int32, (bq, bk), 1)
    causal = col <= row

    def visit(j, carry, masked):
        start = pl.multiple_of(j * bk, bk)
        k_b = k_ref[pl.ds(start, bk), :]
        v_b = v_ref[pl.ds(start, bk), :]
        new = []
        for hh in range(2):
            m, l, acc = carry[hh]
            sl = slice(hh * LANES, (hh + 1) * LANES)
            s = _dot_nt(q[:, sl], k_b[:, sl])
            if masked:
                s = jnp.where(causal, s, -jnp.inf)
            m_new = jnp.maximum(m, jnp.max(s, axis=-1, keepdims=True))
            alpha = jnp.exp(m - m_new)
            p = jnp.exp(s - m_new)
            l = alpha * l + jnp.sum(p, axis=-1, keepdims=True)
            acc = alpha * acc + _dot(p.astype(BF16), v_b)
            new.append((m_new, l, acc))
        return tuple(new)

    init = tuple((jnp.full((bq, 1), -jnp.inf, F32), jnp.zeros((bq, 1), F32), jnp.zeros((bq, LANES), F32))
                 for _ in range(2))
    carry = visit(i, init, True)
    carry = lax.fori_loop(0, i, lambda n, cr: visit(n, cr, False), carry)
    o_ref[...] = jnp.where(lo, carry[0][2] / carry[0][1], carry[1][2] / carry[1][1]).astype(o_ref.dtype)


def _mla_attn(q, k, v, batch):
    t = q.shape[0]
    s = t // batch
    bq = ATTN_BLOCK
    nq = s // bq
    pairs = MLA_HEADS // 2
    return pl.pallas_call(
        _mla_kernel,
        grid=(batch, pairs, nq),
        in_specs=[pl.BlockSpec((bq, 2 * LANES), lambda b, p, i: (b * nq + i, p)),
                  pl.BlockSpec((s, 2 * LANES), lambda b, p, i: (b, p)),
                  pl.BlockSpec((s, LANES), lambda b, p, i: (b, p))],
        out_specs=pl.BlockSpec((bq, LANES), lambda b, p, i: (b * nq + i, p)),
        out_shape=jax.ShapeDtypeStruct((t, MLA_HEADS * MLA_V), BF16),
        compiler_params=pltpu.CompilerParams(dimension_semantics=("arbitrary",) * 3,
                                             vmem_limit_bytes=VMEM_LIMIT),
        name="mla_attn",
    )(q, k, v)


def _post_kernel(h_ref, og_ref, os_ref, om_ref, w_o_ref, g1_ref, b1_ref, w_in_ref, w_out_ref,
                 g2_ref, b2_ref, p_ref, w_g_ref, w_p_ref, out_ref):
    n_g = GDN_HEADS * GDN_DV
    n_s = SB_HEADS * SB_DIM
    x = h_ref[...]
    mix = (_dot(og_ref[...], w_o_ref[:n_g, :]) + _dot(os_ref[...], w_o_ref[n_g:n_g + n_s, :])
           + _dot(om_ref[...], w_o_ref[n_g + n_s:, :]))
    h1 = _layer_norm(DEEPNORM_ALPHA * x + mix, g1_ref[...], b1_ref[...])
    y = _swiglu(h1.astype(BF16), w_in_ref, w_out_ref)
    h2 = _layer_norm(DEEPNORM_ALPHA * h1 + 0.5 * y, g2_ref[...], b2_ref[...])
    gate = _sigmoid(_dot(h2.astype(BF16), w_g_ref[...]))
    out_ref[...] = h2 + gate * _dot(p_ref[...].astype(BF16), w_p_ref[...])


def _post(h, o_gdn, o_sb, o_mla, w_o, g1, b1, w_in, w_out, g2, b2, p, w_g, w_p):
    t = h.shape[0]
    tm = TOKEN_TILE
    row = lambda n: pl.BlockSpec((tm, n), lambda i: (i, 0))
    return pl.pallas_call(
        _post_kernel,
        grid=(t // tm,),
        in_specs=[row(D_MODEL), row(o_gdn.shape[1]), row(o_sb.shape[1]), row(o_mla.shape[1]),
                  _resident(w_o.shape), _resident(g1.shape), _resident(b1.shape),
                  _resident(w_in.shape), _resident(w_out.shape), _resident(g2.shape), _resident(b2.shape),
                  row(PLE_DIM), _resident(w_g.shape), _resident(w_p.shape)],
        out_specs=row(D_MODEL),
        out_shape=jax.ShapeDtypeStruct((t, D_MODEL), F32),
        compiler_params=pltpu.CompilerParams(dimension_semantics=("arbitrary",),
                                             vmem_limit_bytes=VMEM_LIMIT),
        name="post",
    )(h, o_gdn, o_sb, o_mla, w_o, g1, b1, w_in, w_out, g2, b2, p, w_g, w_p)


def _mix_columns():
    hd = GDN_HEADS * GDN_DK
    off_q, off_k, off_v, off_z = 0, hd, 2 * hd, 3 * hd
    off_a = 4 * hd
    off_b = off_a + GDN_HEADS
    off_sb = off_b + GDN_HEADS
    off_mq = off_sb + N_SB
    off_ckv = off_mq + MLA_Q_RANK
    off_kr = off_ckv + MLA_KV_RANK
    cols = []
    for h in range(GDN_HEADS):
        cols += list(range(off_q + h * GDN_DK, off_q + (h + 1) * GDN_DK))
        cols += list(range(off_k + h * GDN_DK, off_k + (h + 1) * GDN_DK))
    for h in range(GDN_HEADS):
        cols += list(range(off_v + h * GDN_DV, off_v + (h + 1) * GDN_DV))
        cols += list(range(off_z + h * GDN_DV, off_z + (h + 1) * GDN_DV))
    cols += list(range(off_sb, off_sb + N_SB))
    cols += list(range(off_mq, off_mq + MLA_Q_RANK))
    cols += list(range(off_ckv, off_ckv + MLA_KV_RANK))
    half = MLA_ROPE // 2
    pad = LANES - MLA_NOPE - MLA_ROPE
    m1 = (list(range(off_a, off_a + GDN_HEADS)) + list(range(off_b, off_b + GDN_HEADS))
          + [-1] * (MLA_NOPE - 2 * GDN_HEADS) + list(range(off_kr, off_kr + MLA_ROPE)) + [-1] * pad)
    m2 = ([-1] * MLA_NOPE + list(range(off_kr + half, off_kr + MLA_ROPE))
          + list(range(off_kr, off_kr + half)) + [-1] * pad)
    cols += m1 + m2
    assert len(cols) == N_MIX
    return np.asarray(cols, np.int32)


def _uq_columns():
    d = MLA_NOPE + MLA_ROPE
    half = MLA_ROPE // 2
    pad = LANES - d
    main, swap = [], []
    for h in range(MLA_HEADS):
        main += list(range(h * d, (h + 1) * d)) + [-1] * pad
        swap += ([-1] * MLA_NOPE + list(range(h * d + MLA_NOPE + half, (h + 1) * d))
                 + list(range(h * d + MLA_NOPE, h * d + MLA_NOPE + half)) + [-1] * pad)
    return np.asarray(main + swap, np.int32)


def _ukv_columns():
    d = MLA_NOPE + MLA_V
    keys, vals = [], []
    for h in range(MLA_HEADS):
        keys += list(range(h * d, h * d + MLA_NOPE)) + [-1] * (LANES - MLA_NOPE)
        vals += list(range(h * d + MLA_NOPE, (h + 1) * d))
    return np.asarray(keys + vals, np.int32)


def _take_columns(w, cols):
    picked = jnp.take(w, jnp.asarray(np.maximum(cols, 0)), axis=-1)
    return jnp.where(jnp.asarray(cols >= 0), picked, 0.0)


def _conv_columns(conv_w):
    hd = GDN_HEADS * GDN_DK
    qk, vz = [], []
    for h in range(GDN_HEADS):
        qk += list(range(h * GDN_DK, (h + 1) * GDN_DK)) + list(range(hd + h * GDN_DK, hd + (h + 1) * GDN_DK))
        vz += list(range(2 * hd + h * GDN_DV, 2 * hd + (h + 1) * GDN_DV)) + [-1] * GDN_DV
    return _take_columns(conv_w, np.asarray(qk, np.int32)), _take_columns(conv_w, np.asarray(vz, np.int32))


def _lane_row(vec, start):
    return jnp.zeros((1, LANES), F32).at[0, start:start + vec.shape[0]].set(vec.astype(F32))


def _rope_tables(positions):
    inv = 1.0 / (ROPE_BASE ** (jnp.arange(0, MLA_ROPE, 2, dtype=F32) / MLA_ROPE))
    ang = positions.astype(F32).reshape(-1, 1) * inv
    cos, sin = jnp.cos(ang), jnp.sin(ang)
    t = ang.shape[0]
    pad = LANES - MLA_NOPE - MLA_ROPE
    cos_t = jnp.concatenate([jnp.ones((t, MLA_NOPE), F32), cos, cos, jnp.zeros((t, pad), F32)], axis=-1)
    sin_t = jnp.concatenate([jnp.zeros((t, MLA_NOPE), F32), -sin, sin, jnp.zeros((t, pad), F32)], axis=-1)
    return jnp.concatenate([cos_t, sin_t], axis=-1)


def kernel(x, p, positions, ffa_w_in, ffa_w_out, mix_w_in, gdn_conv_w, gdn_a_log, gdn_dt_bias, gdn_norm_w, mla_q_norm_w, mla_kv_norm_w, mla_w_uq, mla_w_ukv, mix_w_o, ffb_w_in, ffb_w_out, ln_g, ln_b, ple_w_gate, ple_w_proj):
    batch, seq, d = x.shape
    t = batch * seq
    depth = p.shape[0]
    cs = _rope_tables(positions)
    mix_cols, uq_cols, ukv_cols = _mix_columns(), _uq_columns(), _ukv_columns()
    h = x.reshape(t, d)
    for i in range(depth):
        w_mix = _take_columns(mix_w_in[i], mix_cols).astype(BF16)
        cw_qk, cw_vz = _conv_columns(gdn_conv_w[i])
        row = lambda v: v.reshape(1, -1)
        h, qk, vz, sb, ml = _ffn_inproj(h, ffa_w_in[i].astype(BF16), ffa_w_out[i].astype(BF16),
                                        row(ln_g[i, 0]), row(ln_b[i, 0]), w_mix)
        o_gdn = _gdn(qk, vz, ml, cw_qk, cw_vz, _lane_row(gdn_a_log[i], A_LANE), _lane_row(gdn_dt_bias[i], A_LANE),
                     row(jnp.concatenate([gdn_norm_w[i], gdn_norm_w[i]])), batch)
        q_m, k_m, v_m = _mla_prep(ml, cs, row(mla_q_norm_w[i]), row(mla_kv_norm_w[i]),
                                  _take_columns(mla_w_uq[i], uq_cols).astype(BF16),
                                  _take_columns(mla_w_ukv[i], ukv_cols).astype(BF16))
        o_sb = _sb_attn(sb, batch)
        o_mla = _mla_attn(q_m, k_m, v_m, batch)
        h = _post(h, o_gdn, o_sb, o_mla, mix_w_o[i].astype(BF16), row(ln_g[i, 1]), row(ln_b[i, 1]),
                  ffb_w_in[i].astype(BF16), ffb_w_out[i].astype(BF16), row(ln_g[i, 2]), row(ln_b[i, 2]),
                  p[i].reshape(t, -1), ple_w_gate[i].astype(BF16), ple_w_proj[i].astype(BF16))
    return h.reshape(batch, seq, d)
```

```python
import functools

import numpy as np
import jax
import jax.numpy as jnp
from jax import lax
from jax.experimental import pallas as pl
from jax.experimental.pallas import tpu as pltpu

F32 = jnp.float32
BF16 = jnp.bfloat16

DEPTH = 2
D_MODEL = 1024
PLE_DIM = 256
D_FF = 2816
LN_EPS = 1e-5
RMS_EPS = 1e-6
DEEPNORM_ALPHA = (2 * DEPTH) ** 0.25

GDN_HEADS = 8
GDN_DK = 64
GDN_DV = 64
GDN_CONV = 4
SB_HEADS = 4
SB_DIM = 64
MLA_HEADS = 4
MLA_NOPE = 64
MLA_ROPE = 32
MLA_V = 64
MLA_Q_RANK = 256
MLA_KV_RANK = 128
ROPE_BASE = 10000.0

LANES = 128
HALF = LANES // 2
VMEM_LIMIT = 56 * 1024 * 1024

GDN_CHUNK = 128
TOKEN_TILE = 256
SB_QUERY_BLOCK = 512
MLA_BLOCK = 512

N_QK = GDN_HEADS * LANES
N_VZ = GDN_HEADS * LANES
N_SB = 3 * SB_HEADS * SB_DIM
N_ML = MLA_Q_RANK + MLA_KV_RANK + 2 * LANES
ML_M1 = (MLA_Q_RANK + MLA_KV_RANK) // LANES
A_LANE = 0
B_LANE = GDN_HEADS
N_MIX = N_QK + N_VZ + N_SB + N_ML


def _silu(x):
    return x / (1.0 + jnp.exp(-x))


def _sigmoid(x):
    return 1.0 / (1.0 + jnp.exp(-x))


def _softplus(x):
    return jnp.maximum(x, 0.0) + jnp.log1p(jnp.exp(-jnp.abs(x)))


def _layer_norm(r, g, b):
    mu = jnp.mean(r, axis=-1, keepdims=True)
    d = r - mu
    var = jnp.mean(d * d, axis=-1, keepdims=True)
    return d * lax.rsqrt(var + LN_EPS) * g + b


def _dot(a, b):
    return jnp.dot(a, b, preferred_element_type=F32)


def _dot_nt(a, b):
    return lax.dot_general(a, b, (((1,), (1,)), ((), ())), preferred_element_type=F32)


def _head_rows(x, lo):
    zero = jnp.zeros_like(x)
    return jnp.concatenate([jnp.where(lo, x, zero), jnp.where(lo, zero, x)], axis=0)


def _split2(x):
    hi = x.astype(BF16)
    lo = (x - hi.astype(F32)).astype(BF16)
    return hi, lo


def _split3(x):
    hi = x.astype(BF16)
    r = x - hi.astype(F32)
    mid = r.astype(BF16)
    lo = (r - mid.astype(F32)).astype(BF16)
    return hi, mid, lo


def _swiglu(xb, w_in_ref, w_out_ref):
    gu = _dot(xb, w_in_ref[...])
    act = _silu(gu[:, :D_FF]) * gu[:, D_FF:]
    return _dot(act.astype(BF16), w_out_ref[...])


def _resident(shape):
    return pl.BlockSpec(shape, lambda *_: (0,) * len(shape), pipeline_mode=pl.Buffered(1))


def _ffn_inproj_kernel(h_ref, w_in_ref, w_out_ref, g_ref, b_ref, w_mix_ref,
                       h_out, qk_out, vz_out, sb_out, ml_out):
    x = h_ref[...]
    y = _swiglu(x.astype(BF16), w_in_ref, w_out_ref)
    hn = _layer_norm(DEEPNORM_ALPHA * x + 0.5 * y, g_ref[...], b_ref[...])
    h_out[...] = hn
    proj = _dot(hn.astype(BF16), w_mix_ref[...])
    qk_out[...] = proj[:, :N_QK]
    vz_out[...] = proj[:, N_QK:N_QK + N_VZ]
    sb_out[...] = proj[:, N_QK + N_VZ:N_QK + N_VZ + N_SB].astype(BF16)
    ml_out[...] = proj[:, N_QK + N_VZ + N_SB:]


def _ffn_inproj(h, w_in, w_out, g, b, w_mix):
    t = h.shape[0]
    tm = TOKEN_TILE
    row = lambda n: pl.BlockSpec((tm, n), lambda i: (i, 0))
    return pl.pallas_call(
        _ffn_inproj_kernel,
        grid=(t // tm,),
        in_specs=[row(D_MODEL), _resident(w_in.shape), _resident(w_out.shape),
                  _resident(g.shape), _resident(b.shape), _resident(w_mix.shape)],
        out_specs=[row(D_MODEL), row(N_QK), row(N_VZ), row(N_SB), row(N_ML)],
        out_shape=[jax.ShapeDtypeStruct((t, D_MODEL), F32),
                   jax.ShapeDtypeStruct((t, N_QK), F32),
                   jax.ShapeDtypeStruct((t, N_VZ), F32),
                   jax.ShapeDtypeStruct((t, N_SB), BF16),
                   jax.ShapeDtypeStruct((t, N_ML), F32)],
        compiler_params=pltpu.CompilerParams(dimension_semantics=("arbitrary",),
                                             vmem_limit_bytes=VMEM_LIMIT),
        name="ffn_inproj",
    )(h, w_in, w_out, g, b, w_mix)


def _causal_conv(x_ref, w_ref, buf_ref, first):
    c = x_ref.shape[0]

    @pl.when(first)
    def _():
        buf_ref[0:8, :] = jnp.zeros((8, buf_ref.shape[1]), F32)

    x = x_ref[...]
    buf_ref[8:8 + c, :] = x
    y = x * w_ref[GDN_CONV - 1:GDN_CONV, :]
    for j in range(GDN_CONV - 1):
        off = 8 - (GDN_CONV - 1) + j
        y = y + buf_ref[off:off + c, :] * w_ref[j:j + 1, :]
    buf_ref[0:8, :] = x[c - 8:, :]
    return y


def _unit_lower_inverse(l_strict, row, col):
    c = l_strict.shape[0]
    blk = lambda v, log2: jnp.right_shift(v, log2)
    pair = (blk(row, 1) == blk(col, 1)) & (row > col)
    x = jnp.where(row == col, 1.0, 0.0) - jnp.where(pair, l_strict, 0.0)
    log2 = 1
    while (2 << log2) <= c:
        off = (blk(row, log2 + 1) == blk(col, log2 + 1)) & (blk(row, log2) > blk(col, log2))
        l_off = jnp.where(off, l_strict, 0.0).astype(BF16)
        xb = x.astype(BF16)
        x = x - _dot(xb, _dot(l_off, xb).astype(BF16))
        log2 += 1
    return x


def _gdn_kernel(qk_ref, vz_ref, m1_ref, cw_qk_ref, cw_vz_ref, alog_ref, dtb_ref, nw_ref,
                o_ref, buf_qk, buf_vz, state_ref):
    c = qk_ref.shape[0]
    first = pl.program_id(1) == 0

    @pl.when(first)
    def _():
        state_ref[...] = jnp.zeros(state_ref.shape, F32)

    lane = lax.broadcasted_iota(jnp.int32, (1, LANES), 1)
    lo = lane < HALF
    row = lax.broadcasted_iota(jnp.int32, (c, c), 0)
    col = lax.broadcasted_iota(jnp.int32, (c, c), 1)
    incl = row >= col
    strict = row > col
    tril = jnp.where(incl, 1.0, 0.0).astype(BF16)

    qk = _silu(_causal_conv(qk_ref, cw_qk_ref, buf_qk, first))
    vz_raw = vz_ref[...]
    vconv = _causal_conv(vz_ref, cw_vz_ref, buf_vz, first)
    m1 = m1_ref[...]
    g_all = -jnp.exp(alog_ref[...]) * _softplus(m1 + dtb_ref[...])
    beta_all = _sigmoid(m1)
    g1, g2, g3 = _split3(g_all)
    gc_all = _dot(tril, g1) + _dot(tril, g2) + _dot(tril, g3)
    gc_rows = gc_all.T

    outs = []
    for h in range(GDN_HEADS):
        sl = slice(h * LANES, (h + 1) * LANES)
        x = qk[:, sl]
        sq = x * x
        ss_q = jnp.sum(jnp.where(lo, sq, 0.0), axis=-1, keepdims=True)
        ss_k = jnp.sum(jnp.where(lo, 0.0, sq), axis=-1, keepdims=True)
        xn = x * jnp.where(lo, lax.rsqrt(ss_q + RMS_EPS) * (GDN_DK ** -0.5), lax.rsqrt(ss_k + RMS_EPS))
        q_lo = jnp.where(lo, xn, 0.0)
        k_lo = jnp.where(lo, pltpu.roll(xn, HALF, axis=1), 0.0)
        vz = jnp.where(lo, _silu(vconv[:, sl]), 0.0)

        gc_col = gc_all[:, A_LANE + h:A_LANE + h + 1]
        gc_row = gc_rows[A_LANE + h:A_LANE + h + 1, :]
        gc_last = gc_row[:, c - 1:c]
        beta = beta_all[:, B_LANE + h:B_LANE + h + 1]
        decay = jnp.where(incl, jnp.exp(jnp.where(incl, gc_col - gc_row, 0.0)), 0.0)
        e_col = jnp.exp(gc_col)

        k_b = k_lo.astype(BF16)
        gram = _dot_nt(jnp.concatenate([q_lo, k_lo], axis=0).astype(BF16), k_b)
        qk_m = gram[:c] * decay
        l_strict = jnp.where(strict, gram[c:] * decay * beta, 0.0)
        t_inv = _unit_lower_inverse(l_strict, row, col).astype(BF16)
        u = _dot(t_inv, (vz * beta).astype(BF16))
        w = _dot(t_inv, (k_lo * (beta * e_col)).astype(BF16))

        s_old = state_ref[h]
        s_b = s_old.astype(BF16)
        v_new = u - _dot(w.astype(BF16), s_b)
        v_b = v_new.astype(BF16)
        o = _dot((q_lo * e_col).astype(BF16), s_b) + _dot(qk_m.astype(BF16), v_b)
        k_dec = k_lo * jnp.exp(gc_last - gc_col)
        state_ref[h] = s_old * jnp.exp(gc_last) + _dot(k_dec.T.astype(BF16), v_b)

        ms = jnp.sum(o * o, axis=-1, keepdims=True) * (1.0 / GDN_DV)
        z = pltpu.roll(vz_raw[:, sl], HALF, axis=1)
        outs.append(o * lax.rsqrt(ms + RMS_EPS) * nw_ref[...] * _silu(z))

    for p in range(GDN_HEADS // 2):
        pair = jnp.where(lo, outs[2 * p], pltpu.roll(outs[2 * p + 1], HALF, axis=1))
        o_ref[:, p * LANES:(p + 1) * LANES] = pair.astype(o_ref.dtype)


def _gdn(qk, vz, ml, cw_qk, cw_vz, alog, dtb, nw, batch):
    t = qk.shape[0]
    c = GDN_CHUNK
    nc = t // batch // c
    blk = lambda n, j: pl.BlockSpec((c, n), lambda b, i: (b * nc + i, j))
    small = lambda a: pl.BlockSpec(a.shape, lambda b, i: (0, 0))
    return pl.pallas_call(
        _gdn_kernel,
        grid=(batch, nc),
        in_specs=[blk(N_QK, 0), blk(N_VZ, 0), blk(LANES, ML_M1),
                  small(cw_qk), small(cw_vz), small(alog), small(dtb), small(nw)],
        out_specs=blk(GDN_HEADS * GDN_DV, 0),
        out_shape=jax.ShapeDtypeStruct((t, GDN_HEADS * GDN_DV), BF16),
        scratch_shapes=[pltpu.VMEM((c + 8, N_QK), F32), pltpu.VMEM((c + 8, N_VZ), F32),
                        pltpu.VMEM((GDN_HEADS, LANES, LANES), F32)],
        compiler_params=pltpu.CompilerParams(dimension_semantics=("arbitrary", "arbitrary"),
                                             vmem_limit_bytes=VMEM_LIMIT),
        name="gdn",
    )(qk, vz, ml, cw_qk, cw_vz, alog, dtb, nw)


def _mla_prep_kernel(ml_ref, cs_ref, qnw_ref, kvnw_ref, w_uq_ref, w_ukv_ref, q_out, k_out, v_out):
    nq = MLA_HEADS * LANES
    ml = ml_ref[...]
    mq = ml[:, :MLA_Q_RANK]
    ckv = ml[:, MLA_Q_RANK:MLA_Q_RANK + MLA_KV_RANK]
    m1 = ml[:, ML_M1 * LANES:(ML_M1 + 1) * LANES]
    m2 = ml[:, (ML_M1 + 1) * LANES:]
    cos = cs_ref[:, :LANES]
    sin = cs_ref[:, LANES:]
    lane = lax.broadcasted_iota(jnp.int32, (1, LANES), 1)
    rot = (lane >= MLA_NOPE) & (lane < MLA_NOPE + MLA_ROPE)

    qn = mq * lax.rsqrt(jnp.mean(mq * mq, axis=-1, keepdims=True) + RMS_EPS) * qnw_ref[...]
    qf = _dot(qn.astype(BF16), w_uq_ref[...])
    scale = (MLA_NOPE + MLA_ROPE) ** -0.5
    kvn = ckv * lax.rsqrt(jnp.mean(ckv * ckv, axis=-1, keepdims=True) + RMS_EPS) * kvnw_ref[...]
    kv = _dot(kvn.astype(BF16), w_ukv_ref[...])
    k_rot = jnp.where(rot, m1 * cos, 0.0) + m2 * sin
    for h in range(MLA_HEADS):
        sl = slice(h * LANES, (h + 1) * LANES)
        q_h = qf[:, sl] * cos + qf[:, nq + h * LANES:nq + (h + 1) * LANES] * sin
        q_out[:, sl] = (q_h * scale).astype(BF16)
        k_out[:, sl] = (kv[:, sl] + k_rot).astype(BF16)
    v_out[...] = kv[:, nq:].astype(BF16)


def _mla_prep(ml, cs, qnw, kvnw, w_uq, w_ukv):
    t = ml.shape[0]
    tm = TOKEN_TILE
    row = lambda n: pl.BlockSpec((tm, n), lambda i: (i, 0))
    nq = MLA_HEADS * LANES
    nv = MLA_HEADS * MLA_V
    return pl.pallas_call(
        _mla_prep_kernel,
        grid=(t // tm,),
        in_specs=[row(N_ML), row(2 * LANES), _resident(qnw.shape), _resident(kvnw.shape),
                  _resident(w_uq.shape), _resident(w_ukv.shape)],
        out_specs=[row(nq), row(nq), row(nv)],
        out_shape=[jax.ShapeDtypeStruct((t, nq), BF16), jax.ShapeDtypeStruct((t, nq), BF16),
                   jax.ShapeDtypeStruct((t, nv), BF16)],
        compiler_params=pltpu.CompilerParams(dimension_semantics=("arbitrary",),
                                             vmem_limit_bytes=VMEM_LIMIT),
        name="mla_prep",
    )(ml, cs, qnw, kvnw, w_uq, w_ukv)


def _sb_kernel(q_ref, k_ref, v_ref, o_ref):
    bq = q_ref.shape[0]
    g = LANES
    per = bq // g
    i = pl.program_id(2)
    lane = lax.broadcasted_iota(jnp.int32, (1, LANES), 1)
    lo = lane < HALF
    q = q_ref[...] * jnp.asarray(SB_DIM ** -0.5, BF16)
    row = lax.broadcasted_iota(jnp.int32, (bq, g), 0)
    col = lax.broadcasted_iota(jnp.int32, (bq, g), 1)
    jr = lax.broadcasted_iota(jnp.int32, (g, g), 0)
    jc = lax.broadcasted_iota(jnp.int32, (g, g), 1)
    later = jnp.where(jr > jc, 1.0, 0.0).astype(BF16)
    suffix_total = jnp.concatenate([later, jnp.ones((g, g), BF16)], axis=1)
    suffix_total = jnp.concatenate([suffix_total, suffix_total], axis=0)

    def group(start, run, acc, mask):
        k_g = k_ref[pl.ds(start, g), :]
        v_g = v_ref[pl.ds(start, g), :]
        z = _dot_nt(q, _head_rows(k_g, lo))
        log_1m = jnp.minimum(-z, 0.0) - jnp.log(1.0 + jnp.exp(-jnp.abs(z)))
        if mask is not None:
            log_1m = jnp.where(mask, log_1m, 0.0)
        wts, new_run = [], []
        for hh in range(2):
            sl = slice(hh * g, (hh + 1) * g)
            hi, low = _split2(log_1m[:, sl])
            st = _dot(jnp.concatenate([hi, low], axis=1), suffix_total)
            wts.append(jnp.exp(z[:, sl] + log_1m[:, sl] + run[hh] + st[:, :g]))
            new_run.append(run[hh] + st[:, g:])
        wts = jnp.concatenate(wts, axis=1)
        if mask is not None:
            wts = jnp.where(mask, wts, 0.0)
        acc = acc + _dot(wts.astype(BF16), _head_rows(v_g, lo))
        return tuple(new_run), acc

    run = (jnp.zeros((bq, g), F32), jnp.zeros((bq, g), F32))
    acc = jnp.zeros((bq, LANES), F32)
    base = i * bq
    for d in range(per):
        off = (per - 1 - d) * g
        mask = (col + off) < row
        run, acc = group(pl.multiple_of(base + off, g), run, acc, jnp.concatenate([mask, mask], axis=1))

    def block(n, carry):
        run, acc = carry
        start = (i - 1 - n) * bq
        for d in range(per):
            run, acc = group(pl.multiple_of(start + (per - 1 - d) * g, g), run, acc, None)
        return run, acc

    run, acc = lax.fori_loop(0, i, block, (run, acc))
    o_ref[...] = acc.astype(o_ref.dtype)


def _sb_attn(sb, batch):
    t = sb.shape[0]
    s = t // batch
    bq = SB_QUERY_BLOCK
    nq = s // bq
    pairs = SB_HEADS // 2
    return pl.pallas_call(
        _sb_kernel,
        grid=(batch, pairs, nq),
        in_specs=[pl.BlockSpec((bq, LANES), lambda b, p, i: (b * nq + i, p)),
                  pl.BlockSpec((s, LANES), lambda b, p, i: (b, pairs + p)),
                  pl.BlockSpec((s, LANES), lambda b, p, i: (b, 2 * pairs + p))],
        out_specs=pl.BlockSpec((bq, LANES), lambda b, p, i: (b * nq + i, p)),
        out_shape=jax.ShapeDtypeStruct((t, SB_HEADS * SB_DIM), BF16),
        compiler_params=pltpu.CompilerParams(dimension_semantics=("arbitrary",) * 3,
                                             vmem_limit_bytes=VMEM_LIMIT),
        name="sb_attn",
    )(sb, sb, sb)


def _mla_kernel(q_ref, k_ref, v_ref, o_ref):
    bq = q_ref.shape[0]
    bk = bq
    i = pl.program_id(2)
    lane = lax.broadcasted_iota(jnp.int32, (1, LANES), 1)
    lo = lane < HALF
    q = q_ref[...]
    row = lax.broadcasted_iota(jnp.int32, (bq, bk), 0)
    col = lax.broadcasted_iota(jnp.int32, (bq, bk), 1)
    causal = col <= row

    def visit(j, carry, masked):
        stats, acc = carry
        start = pl.multiple_of(j * bk, bk)
        k_b = k_ref[pl.ds(start, bk), :]
        v_b = v_ref[pl.ds(start, bk), :]
        new, probs, alphas = [], [], []
        for hh in range(2):
            m, l = stats[hh]
            sl = slice(hh * LANES, (hh + 1) * LANES)
            s = _dot_nt(q[:, sl], k_b[:, sl])
            if masked:
                s = jnp.where(causal, s, -jnp.inf)
            m_new = jnp.maximum(m, jnp.max(s, axis=-1, keepdims=True))
            alpha = jnp.exp(m - m_new)
            p = jnp.exp(s - m_new)
            new.append((m_new, alpha * l + jnp.sum(p, axis=-1, keepdims=True)))
            probs.append(p.astype(BF16))
            alphas.append(alpha)
        pv = _dot(jnp.concatenate(probs, axis=1), _head_rows(v_b, lo))
        return tuple(new), jnp.where(lo, alphas[0], alphas[1]) * acc + pv

    init = (tuple((jnp.full((bq, 1), -jnp.inf, F32), jnp.zeros((bq, 1), F32)) for _ in range(2)),
            jnp.zeros((bq, LANES), F32))
    carry = visit(i, init, True)
    stats, acc = lax.fori_loop(0, i, lambda n, cr: visit(n, cr, False), carry)
    o_ref[...] = (acc / jnp.where(lo, stats[0][1], stats[1][1])).astype(o_ref.dtype)


def _mla_attn(q, k, v, batch):
    t = q.shape[0]
    s = t // batch
    bq = MLA_BLOCK
    nq = s // bq
    pairs = MLA_HEADS // 2
    return pl.pallas_call(
        _mla_kernel,
        grid=(batch, pairs, nq),
        in_specs=[pl.BlockSpec((bq, 2 * LANES), lambda b, p, i: (b * nq + i, p)),
                  pl.BlockSpec((s, 2 * LANES), lambda b, p, i: (b, p)),
                  pl.BlockSpec((s, LANES), lambda b, p, i: (b, p))],
        out_specs=pl.BlockSpec((bq, LANES), lambda b, p, i: (b * nq + i, p)),
        out_shape=jax.ShapeDtypeStruct((t, MLA_HEADS * MLA_V), BF16),
        compiler_params=pltpu.CompilerParams(dimension_semantics=("arbitrary",) * 3,
                                             vmem_limit_bytes=VMEM_LIMIT),
        name="mla_attn",
    )(q, k, v)


def _post_kernel(h_ref, og_ref, os_ref, om_ref, w_o_ref, g1_ref, b1_ref, w_in_ref, w_out_ref,
                 g2_ref, b2_ref, p_ref, w_g_ref, w_p_ref, out_ref):
    n_g = GDN_HEADS * GDN_DV
    n_s = SB_HEADS * SB_DIM
    x = h_ref[...]
    mix = (_dot(og_ref[...], w_o_ref[:n_g, :]) + _dot(os_ref[...], w_o_ref[n_g:n_g + n_s, :])
           + _dot(om_ref[...], w_o_ref[n_g + n_s:, :]))
    h1 = _layer_norm(DEEPNORM_ALPHA * x + mix, g1_ref[...], b1_ref[...])
    y = _swiglu(h1.astype(BF16), w_in_ref, w_out_ref)
    h2 = _layer_norm(DEEPNORM_ALPHA * h1 + 0.5 * y, g2_ref[...], b2_ref[...])
    gate = _sigmoid(_dot(h2.astype(BF16), w_g_ref[...]))
    out_ref[...] = h2 + gate * _dot(p_ref[...].astype(BF16), w_p_ref[...])


def _post(h, o_gdn, o_sb, o_mla, w_o, g1, b1, w_in, w_out, g2, b2, p, w_g, w_p):
    t = h.shape[0]
    tm = TOKEN_TILE
    row = lambda n: pl.BlockSpec((tm, n), lambda i: (i, 0))
    return pl.pallas_call(
        _post_kernel,
        grid=(t // tm,),
        in_specs=[row(D_MODEL), row(o_gdn.shape[1]), row(o_sb.shape[1]), row(o_mla.shape[1]),
                  _resident(w_o.shape), _resident(g1.shape), _resident(b1.shape),
                  _resident(w_in.shape), _resident(w_out.shape), _resident(g2.shape), _resident(b2.shape),
                  row(PLE_DIM), _resident(w_g.shape), _resident(w_p.shape)],
        out_specs=row(D_MODEL),
        out_shape=jax.ShapeDtypeStruct((t, D_MODEL), F32),
        compiler_params=pltpu.CompilerParams(dimension_semantics=("arbitrary",),
                                             vmem_limit_bytes=VMEM_LIMIT),
        name="post",
    )(h, o_gdn, o_sb, o_mla, w_o, g1, b1, w_in, w_out, g2, b2, p, w_g, w_p)


def _mix_columns():
    hd = GDN_HEADS * GDN_DK
    off_q, off_k, off_v, off_z = 0, hd, 2 * hd, 3 * hd
    off_a = 4 * hd
    off_b = off_a + GDN_HEADS
    off_sb = off_b + GDN_HEADS
    off_mq = off_sb + N_SB
    off_ckv = off_mq + MLA_Q_RANK
    off_kr = off_ckv + MLA_KV_RANK
    cols = []
    for h in range(GDN_HEADS):
        cols += list(range(off_q + h * GDN_DK, off_q + (h + 1) * GDN_DK))
        cols += list(range(off_k + h * GDN_DK, off_k + (h + 1) * GDN_DK))
    for h in range(GDN_HEADS):
        cols += list(range(off_v + h * GDN_DV, off_v + (h + 1) * GDN_DV))
        cols += list(range(off_z + h * GDN_DV, off_z + (h + 1) * GDN_DV))
    cols += list(range(off_sb, off_sb + N_SB))
    cols += list(range(off_mq, off_mq + MLA_Q_RANK))
    cols += list(range(off_ckv, off_ckv + MLA_KV_RANK))
    half = MLA_ROPE // 2
    pad = LANES - MLA_NOPE - MLA_ROPE
    m1 = (list(range(off_a, off_a + GDN_HEADS)) + list(range(off_b, off_b + GDN_HEADS))
          + [-1] * (MLA_NOPE - 2 * GDN_HEADS) + list(range(off_kr, off_kr + MLA_ROPE)) + [-1] * pad)
    m2 = ([-1] * MLA_NOPE + list(range(off_kr + half, off_kr + MLA_ROPE))
          + list(range(off_kr, off_kr + half)) + [-1] * pad)
    cols += m1 + m2
    assert len(cols) == N_MIX
    return np.asarray(cols, np.int32)


def _uq_columns():
    d = MLA_NOPE + MLA_ROPE
    half = MLA_ROPE // 2
    pad = LANES - d
    main, swap = [], []
    for h in range(MLA_HEADS):
        main += list(range(h * d, (h + 1) * d)) + [-1] * pad
        swap += ([-1] * MLA_NOPE + list(range(h * d + MLA_NOPE + half, (h + 1) * d))
                 + list(range(h * d + MLA_NOPE, h * d + MLA_NOPE + half)) + [-1] * pad)
    return np.asarray(main + swap, np.int32)


def _ukv_columns():
    d = MLA_NOPE + MLA_V
    keys, vals = [], []
    for h in range(MLA_HEADS):
        keys += list(range(h * d, h * d + MLA_NOPE)) + [-1] * (LANES - MLA_NOPE)
        vals += list(range(h * d + MLA_NOPE, (h + 1) * d))
    return np.asarray(keys + vals, np.int32)


def _take_columns(w, cols):
    picked = jnp.take(w, jnp.asarray(np.maximum(cols, 0)), axis=-1)
    return jnp.where(jnp.asarray(cols >= 0), picked, 0.0)


def _conv_columns(conv_w):
    hd = GDN_HEADS * GDN_DK
    qk, vz = [], []
    for h in range(GDN_HEADS):
        qk += list(range(h * GDN_DK, (h + 1) * GDN_DK)) + list(range(hd + h * GDN_DK, hd + (h + 1) * GDN_DK))
        vz += list(range(2 * hd + h * GDN_DV, 2 * hd + (h + 1) * GDN_DV)) + [-1] * GDN_DV
    return _take_columns(conv_w, np.asarray(qk, np.int32)), _take_columns(conv_w, np.asarray(vz, np.int32))


def _lane_row(vec, start):
    return jnp.zeros((1, LANES), F32).at[0, start:start + vec.shape[0]].set(vec.astype(F32))


def _rope_tables(positions):
    inv = 1.0 / (ROPE_BASE ** (jnp.arange(0, MLA_ROPE, 2, dtype=F32) / MLA_ROPE))
    ang = positions.astype(F32).reshape(-1, 1) * inv
    cos, sin = jnp.cos(ang), jnp.sin(ang)
    t = ang.shape[0]
    pad = LANES - MLA_NOPE - MLA_ROPE
    cos_t = jnp.concatenate([jnp.ones((t, MLA_NOPE), F32), cos, cos, jnp.zeros((t, pad), F32)], axis=-1)
    sin_t = jnp.concatenate([jnp.zeros((t, MLA_NOPE), F32), -sin, sin, jnp.zeros((t, pad), F32)], axis=-1)
    return jnp.concatenate([cos_t, sin_t], axis=-1)


def kernel(x, p, positions, ffa_w_in, ffa_w_out, mix_w_in, gdn_conv_w, gdn_a_log, gdn_dt_bias, gdn_norm_w, mla_q_norm_w, mla_kv_norm_w, mla_w_uq, mla_w_ukv, mix_w_o, ffb_w_in, ffb_w_out, ln_g, ln_b, ple_w_gate, ple_w_proj):
    batch, seq, d = x.shape
    t = batch * seq
    depth = p.shape[0]
    cs = _rope_tables(positions)
    mix_cols, uq_cols, ukv_cols = _mix_columns(), _uq_columns(), _ukv_columns()
    h = x.reshape(t, d)
    for i in range(depth):
        w_mix = _take_columns(mix_w_in[i], mix_cols).astype(BF16)
        cw_qk, cw_vz = _conv_columns(gdn_conv_w[i])
        row = lambda v: v.reshape(1, -1)
        h, qk, vz, sb, ml = _ffn_inproj(h, ffa_w_in[i].astype(BF16), ffa_w_out[i].astype(BF16),
                                        row(ln_g[i, 0]), row(ln_b[i, 0]), w_mix)
        o_gdn = _gdn(qk, vz, ml, cw_qk, cw_vz, _lane_row(gdn_a_log[i], A_LANE), _lane_row(gdn_dt_bias[i], A_LANE),
                     row(jnp.concatenate([gdn_norm_w[i], gdn_norm_w[i]])), batch)
        q_m, k_m, v_m = _mla_prep(ml, cs, row(mla_q_norm_w[i]), row(mla_kv_norm_w[i]),
                                  _take_columns(mla_w_uq[i], uq_cols).astype(BF16),
                                  _take_columns(mla_w_ukv[i], ukv_cols).astype(BF16))
        o_sb = _sb_attn(sb, batch)
        o_mla = _mla_attn(q_m, k_m, v_m, batch)
        h = _post(h, o_gdn, o_sb, o_mla, mix_w_o[i].astype(BF16), row(ln_g[i, 1]), row(ln_b[i, 1]),
                  ffb_w_in[i].astype(BF16), ffb_w_out[i].astype(BF16), row(ln_g[i, 2]), row(ln_b[i, 2]),
                  p[i].reshape(t, -1), ple_w_gate[i].astype(BF16), ple_w_proj[i].astype(BF16))
    return h.reshape(batch, seq, d)
```

```python
import functools

import numpy as np
import jax
import jax.numpy as jnp
from jax import lax
from jax.experimental import pallas as pl
from jax.experimental.pallas import tpu as pltpu

F32 = jnp.float32
BF16 = jnp.bfloat16

DEPTH = 2
D_MODEL = 1024
PLE_DIM = 256
D_FF = 2816
LN_EPS = 1e-5
RMS_EPS = 1e-6
DEEPNORM_ALPHA = (2 * DEPTH) ** 0.25

GDN_HEADS = 8
GDN_DK = 64
GDN_DV = 64
GDN_CONV = 4
SB_HEADS = 4
SB_DIM = 64
MLA_HEADS = 4
MLA_NOPE = 64
MLA_ROPE = 32
MLA_V = 64
MLA_Q_RANK = 256
MLA_KV_RANK = 128
ROPE_BASE = 10000.0

LOG2E = 1.4426950408889634
LANES = 128
HALF = LANES // 2
VMEM_LIMIT = 56 * 1024 * 1024

GDN_CHUNK = 128
TOKEN_TILE = 256
SB_QUERY_BLOCK = 1024
MLA_BLOCK = 512

N_QK = GDN_HEADS * LANES
N_VZ = GDN_HEADS * LANES
N_SB = 3 * SB_HEADS * SB_DIM
N_ML = MLA_Q_RANK + MLA_KV_RANK + 2 * LANES
ML_M1 = (MLA_Q_RANK + MLA_KV_RANK) // LANES
A_LANE = 0
B_LANE = GDN_HEADS
N_MIX = N_QK + N_VZ + N_SB + N_ML


def _silu(x):
    return x / (1.0 + jnp.exp(-x))


def _sigmoid(x):
    return 1.0 / (1.0 + jnp.exp(-x))


def _softplus(x):
    return jnp.maximum(x, 0.0) + jnp.log1p(jnp.exp(-jnp.abs(x)))


def _layer_norm(r, g, b):
    mu = jnp.mean(r, axis=-1, keepdims=True)
    d = r - mu
    var = jnp.mean(d * d, axis=-1, keepdims=True)
    return d * lax.rsqrt(var + LN_EPS) * g + b


def _dot(a, b):
    return jnp.dot(a, b, preferred_element_type=F32)


def _dot_nt(a, b):
    return lax.dot_general(a, b, (((1,), (1,)), ((), ())), preferred_element_type=F32)


def _head_rows(x, lo):
    zero = jnp.zeros_like(x)
    return jnp.concatenate([jnp.where(lo, x, zero), jnp.where(lo, zero, x)], axis=0)


def _replace_tail(full, r0, tail):
    return tail if r0 == 0 else jnp.concatenate([full[:r0], tail], axis=0)


def _split2(x):
    hi = x.astype(BF16)
    lo = (x - hi.astype(F32)).astype(BF16)
    return hi, lo


def _split3(x):
    hi = x.astype(BF16)
    r = x - hi.astype(F32)
    mid = r.astype(BF16)
    lo = (r - mid.astype(F32)).astype(BF16)
    return hi, mid, lo


def _swiglu(xb, w_in_ref, w_out_ref):
    gu = _dot(xb, w_in_ref[...])
    act = _silu(gu[:, :D_FF]) * gu[:, D_FF:]
    return _dot(act.astype(BF16), w_out_ref[...])


def _resident(shape):
    return pl.BlockSpec(shape, lambda *_: (0,) * len(shape), pipeline_mode=pl.Buffered(1))


def _ffn_inproj_kernel(h_ref, w_in_ref, w_out_ref, g_ref, b_ref, w_mix_ref,
                       h_out, qk_out, vz_out, sb_out, ml_out):
    x = h_ref[...]
    y = _swiglu(x.astype(BF16), w_in_ref, w_out_ref)
    hn = _layer_norm(DEEPNORM_ALPHA * x + 0.5 * y, g_ref[...], b_ref[...])
    h_out[...] = hn
    proj = _dot(hn.astype(BF16), w_mix_ref[...])
    qk_out[...] = proj[:, :N_QK]
    vz_out[...] = proj[:, N_QK:N_QK + N_VZ]
    n_q = SB_HEADS * SB_DIM
    sb0 = N_QK + N_VZ
    sb_out[:, :n_q] = (proj[:, sb0:sb0 + n_q] * (SB_DIM ** -0.5 * LOG2E)).astype(BF16)
    sb_out[:, n_q:] = proj[:, sb0 + n_q:sb0 + N_SB].astype(BF16)
    ml_out[...] = proj[:, N_QK + N_VZ + N_SB:]


def _ffn_inproj(h, w_in, w_out, g, b, w_mix):
    t = h.shape[0]
    tm = TOKEN_TILE
    row = lambda n: pl.BlockSpec((tm, n), lambda i: (i, 0))
    return pl.pallas_call(
        _ffn_inproj_kernel,
        grid=(t // tm,),
        in_specs=[row(D_MODEL), _resident(w_in.shape), _resident(w_out.shape),
                  _resident(g.shape), _resident(b.shape), _resident(w_mix.shape)],
        out_specs=[row(D_MODEL), row(N_QK), row(N_VZ), row(N_SB), row(N_ML)],
        out_shape=[jax.ShapeDtypeStruct((t, D_MODEL), F32),
                   jax.ShapeDtypeStruct((t, N_QK), F32),
                   jax.ShapeDtypeStruct((t, N_VZ), F32),
                   jax.ShapeDtypeStruct((t, N_SB), BF16),
                   jax.ShapeDtypeStruct((t, N_ML), F32)],
        compiler_params=pltpu.CompilerParams(dimension_semantics=("arbitrary",),
                                             vmem_limit_bytes=VMEM_LIMIT),
        name="ffn_inproj",
    )(h, w_in, w_out, g, b, w_mix)


def _causal_conv(x_ref, w_ref, buf_ref, first):
    c = x_ref.shape[0]

    @pl.when(first)
    def _():
        buf_ref[0:8, :] = jnp.zeros((8, buf_ref.shape[1]), F32)

    x = x_ref[...]
    buf_ref[8:8 + c, :] = x
    y = x * w_ref[GDN_CONV - 1:GDN_CONV, :]
    for j in range(GDN_CONV - 1):
        off = 8 - (GDN_CONV - 1) + j
        y = y + buf_ref[off:off + c, :] * w_ref[j:j + 1, :]
    buf_ref[0:8, :] = x[c - 8:, :]
    return y


def _unit_lower_inverses(l_strict, row, col):
    c = l_strict[0].shape[0]
    blk = lambda v, log2: jnp.right_shift(v, log2)
    pair = (blk(row, 1) == blk(col, 1)) & (row > col)
    eye = jnp.where(row == col, 1.0, 0.0)
    xs = [eye - jnp.where(pair, l, 0.0) for l in l_strict]
    log2 = 1
    while (2 << log2) <= c:
        off = (blk(row, log2 + 1) == blk(col, log2 + 1)) & (blk(row, log2) > blk(col, log2))
        xb = [x.astype(BF16) for x in xs]
        lx = [_dot(jnp.where(off, l, 0.0).astype(BF16), b).astype(BF16) for l, b in zip(l_strict, xb)]
        xs = [x - _dot(b, t) for x, b, t in zip(xs, xb, lx)]
        log2 += 1
    return xs


def _gdn_kernel(qk_ref, vz_ref, m1_ref, cw_qk_ref, cw_vz_ref, alog_ref, dtb_ref, nw_ref,
                o_ref, buf_qk, buf_vz, state_ref):
    c = qk_ref.shape[0]
    first = pl.program_id(1) == 0

    @pl.when(first)
    def _():
        state_ref[...] = jnp.zeros(state_ref.shape, F32)

    lane = lax.broadcasted_iota(jnp.int32, (1, LANES), 1)
    lo = lane < HALF
    row = lax.broadcasted_iota(jnp.int32, (c, c), 0)
    col = lax.broadcasted_iota(jnp.int32, (c, c), 1)
    incl = row >= col
    strict = row > col
    tril = jnp.where(incl, 1.0, 0.0).astype(BF16)

    qk = _silu(_causal_conv(qk_ref, cw_qk_ref, buf_qk, first))
    vz_raw = vz_ref[...]
    vconv = _causal_conv(vz_ref, cw_vz_ref, buf_vz, first)
    m1 = m1_ref[...]
    g_all = -jnp.exp(alog_ref[...]) * _softplus(m1 + dtb_ref[...])
    beta_all = _sigmoid(m1)
    g1, g2, g3 = _split3(g_all)
    gc_all = _dot(tril, g1) + _dot(tril, g2) + _dot(tril, g3)
    gc_rows = gc_all.T

    heads = range(GDN_HEADS)
    sls = [slice(h * LANES, (h + 1) * LANES) for h in heads]
    q_lo, k_lo, v_lo = [], [], []
    for h in heads:
        x = qk[:, sls[h]]
        sq = x * x
        ss_q = jnp.sum(jnp.where(lo, sq, 0.0), axis=-1, keepdims=True)
        ss_k = jnp.sum(jnp.where(lo, 0.0, sq), axis=-1, keepdims=True)
        xn = x * jnp.where(lo, lax.rsqrt(ss_q + RMS_EPS) * (GDN_DK ** -0.5), lax.rsqrt(ss_k + RMS_EPS))
        q_lo.append(jnp.where(lo, xn, 0.0))
        k_lo.append(jnp.where(lo, pltpu.roll(xn, HALF, axis=1), 0.0))
        v_lo.append(jnp.where(lo, _silu(vconv[:, sls[h]]), 0.0))

    gc_col = [gc_all[:, A_LANE + h:A_LANE + h + 1] for h in heads]
    gc_row = [gc_rows[A_LANE + h:A_LANE + h + 1, :] for h in heads]
    gc_last = [r[:, c - 1:c] for r in gc_row]
    beta = [beta_all[:, B_LANE + h:B_LANE + h + 1] for h in heads]
    decay = [jnp.where(incl, jnp.exp(jnp.where(incl, gc_col[h] - gc_row[h], 0.0)), 0.0) for h in heads]
    e_col = [jnp.exp(g) for g in gc_col]

    gram = [_dot_nt(jnp.concatenate([q_lo[h], k_lo[h]], axis=0).astype(BF16), k_lo[h].astype(BF16))
            for h in heads]
    qk_m = [gram[h][:c] * decay[h] for h in heads]
    l_strict = [jnp.where(strict, gram[h][c:] * decay[h] * beta[h], 0.0) for h in heads]
    t_inv = [x.astype(BF16) for x in _unit_lower_inverses(l_strict, row, col)]
    uw = [_dot(t_inv[h], jnp.concatenate([v_lo[h] * beta[h], k_lo[h] * (beta[h] * e_col[h])],
                                         axis=1).astype(BF16)) for h in heads]

    s_old = [state_ref[h] for h in heads]
    ws = [_dot(jnp.concatenate([uw[h][:, LANES:], q_lo[h] * e_col[h]], axis=0).astype(BF16),
               s_old[h].astype(BF16)) for h in heads]
    v_new = [(uw[h][:, :LANES] - ws[h][:c]).astype(BF16) for h in heads]
    k_dec_t = [(k_lo[h] * jnp.exp(gc_last[h] - gc_col[h])).T for h in heads]
    fin = [_dot(jnp.concatenate([qk_m[h], k_dec_t[h]], axis=0).astype(BF16), v_new[h]) for h in heads]
    outs = []
    for h in heads:
        state_ref[h] = s_old[h] * jnp.exp(gc_last[h]) + fin[h][c:]
        o = ws[h][c:] + fin[h][:c]
        ms = jnp.sum(o * o, axis=-1, keepdims=True) * (1.0 / GDN_DV)
        z = pltpu.roll(vz_raw[:, sls[h]], HALF, axis=1)
        outs.append(o * lax.rsqrt(ms + RMS_EPS) * nw_ref[...] * _silu(z))

    for p in range(GDN_HEADS // 2):
        pair = jnp.where(lo, outs[2 * p], pltpu.roll(outs[2 * p + 1], HALF, axis=1))
        o_ref[:, p * LANES:(p + 1) * LANES] = pair.astype(o_ref.dtype)


def _gdn(qk, vz, ml, cw_qk, cw_vz, alog, dtb, nw, batch):
    t = qk.shape[0]
    c = GDN_CHUNK
    nc = t // batch // c
    blk = lambda n, j: pl.BlockSpec((c, n), lambda b, i: (b * nc + i, j))
    small = lambda a: pl.BlockSpec(a.shape, lambda b, i: (0, 0))
    return pl.pallas_call(
        _gdn_kernel,
        grid=(batch, nc),
        in_specs=[blk(N_QK, 0), blk(N_VZ, 0), blk(LANES, ML_M1),
                  small(cw_qk), small(cw_vz), small(alog), small(dtb), small(nw)],
        out_specs=blk(GDN_HEADS * GDN_DV, 0),
        out_shape=jax.ShapeDtypeStruct((t, GDN_HEADS * GDN_DV), BF16),
        scratch_shapes=[pltpu.VMEM((c + 8, N_QK), F32), pltpu.VMEM((c + 8, N_VZ), F32),
                        pltpu.VMEM((GDN_HEADS, LANES, LANES), F32)],
        compiler_params=pltpu.CompilerParams(dimension_semantics=("arbitrary", "arbitrary"),
                                             vmem_limit_bytes=VMEM_LIMIT),
        name="gdn",
    )(qk, vz, ml, cw_qk, cw_vz, alog, dtb, nw)


def _mla_prep_kernel(ml_ref, cs_ref, qnw_ref, kvnw_ref, w_uq_ref, w_ukv_ref, q_out, k_out, v_out):
    nq = MLA_HEADS * LANES
    ml = ml_ref[...]
    mq = ml[:, :MLA_Q_RANK]
    ckv = ml[:, MLA_Q_RANK:MLA_Q_RANK + MLA_KV_RANK]
    m1 = ml[:, ML_M1 * LANES:(ML_M1 + 1) * LANES]
    m2 = ml[:, (ML_M1 + 1) * LANES:]
    cos = cs_ref[:, :LANES]
    sin = cs_ref[:, LANES:]
    lane = lax.broadcasted_iota(jnp.int32, (1, LANES), 1)
    rot = (lane >= MLA_NOPE) & (lane < MLA_NOPE + MLA_ROPE)

    qn = mq * lax.rsqrt(jnp.mean(mq * mq, axis=-1, keepdims=True) + RMS_EPS) * qnw_ref[...]
    qf = _dot(qn.astype(BF16), w_uq_ref[...])
    scale = (MLA_NOPE + MLA_ROPE) ** -0.5 * LOG2E
    kvn = ckv * lax.rsqrt(jnp.mean(ckv * ckv, axis=-1, keepdims=True) + RMS_EPS) * kvnw_ref[...]
    kv = _dot(kvn.astype(BF16), w_ukv_ref[...])
    k_rot = jnp.where(rot, m1 * cos, 0.0) + m2 * sin
    for h in range(MLA_HEADS):
        sl = slice(h * LANES, (h + 1) * LANES)
        q_h = qf[:, sl] * cos + qf[:, nq + h * LANES:nq + (h + 1) * LANES] * sin
        q_out[:, sl] = (q_h * scale).astype(BF16)
        k_out[:, sl] = (kv[:, sl] + k_rot).astype(BF16)
    v_out[...] = kv[:, nq:].astype(BF16)


def _mla_prep(ml, cs, qnw, kvnw, w_uq, w_ukv):
    t = ml.shape[0]
    tm = TOKEN_TILE
    row = lambda n: pl.BlockSpec((tm, n), lambda i: (i, 0))
    nq = MLA_HEADS * LANES
    nv = MLA_HEADS * MLA_V
    return pl.pallas_call(
        _mla_prep_kernel,
        grid=(t // tm,),
        in_specs=[row(N_ML), row(2 * LANES), _resident(qnw.shape), _resident(kvnw.shape),
                  _resident(w_uq.shape), _resident(w_ukv.shape)],
        out_specs=[row(nq), row(nq), row(nv)],
        out_shape=[jax.ShapeDtypeStruct((t, nq), BF16), jax.ShapeDtypeStruct((t, nq), BF16),
                   jax.ShapeDtypeStruct((t, nv), BF16)],
        compiler_params=pltpu.CompilerParams(dimension_semantics=("arbitrary",),
                                             vmem_limit_bytes=VMEM_LIMIT),
        name="mla_prep",
    )(ml, cs, qnw, kvnw, w_uq, w_ukv)


def _sb_kernel(q_ref, k_ref, v_ref, o_ref):
    bq = q_ref.shape[0]
    g = LANES
    per = bq // g
    i = pl.program_id(2)
    lane = lax.broadcasted_iota(jnp.int32, (1, LANES), 1)
    lo = lane < HALF
    q = q_ref[...]
    row = lax.broadcasted_iota(jnp.int32, (bq, g), 0)
    col = lax.broadcasted_iota(jnp.int32, (bq, g), 1)
    jr = lax.broadcasted_iota(jnp.int32, (g, g), 0)
    jc = lax.broadcasted_iota(jnp.int32, (g, g), 1)
    later = jnp.where(jr > jc, 1.0, 0.0).astype(BF16)
    suffix_total = jnp.concatenate([later, jnp.ones((g, g), BF16)], axis=1)
    suffix_total = jnp.concatenate([suffix_total, suffix_total], axis=0)

    def group(start, run, acc, mask, r0):
        k_g = k_ref[pl.ds(start, g), :]
        v_g = v_ref[pl.ds(start, g), :]
        z = _dot_nt(q[r0:], _head_rows(k_g, lo))
        nl = jnp.maximum(z, 0.0) + jnp.log(1.0 + jnp.exp2(-jnp.abs(z))) * LOG2E
        if mask is not None:
            nl = jnp.where(mask, nl, 0.0)
        wts, new_run = [], []
        for hh in range(2):
            sl = slice(hh * g, (hh + 1) * g)
            hi, low = _split2(nl[:, sl])
            st = _dot(jnp.concatenate([hi, low], axis=1), suffix_total)
            wts.append(jnp.exp2(z[:, sl] - nl[:, sl] - (run[hh][r0:] + st[:, :g])))
            new_run.append(_replace_tail(run[hh], r0, run[hh][r0:] + st[:, g:]))
        wts = jnp.concatenate(wts, axis=1)
        if mask is not None:
            wts = jnp.where(mask, wts, 0.0)
        pv = _dot(wts.astype(BF16), _head_rows(v_g, lo))
        return tuple(new_run), _replace_tail(acc, r0, acc[r0:] + pv)

    run = (jnp.zeros((bq, g), F32), jnp.zeros((bq, g), F32))
    acc = jnp.zeros((bq, LANES), F32)
    base = i * bq
    for d in range(per):
        off = (per - 1 - d) * g
        mask = ((col + off) < row)[off:]
        run, acc = group(pl.multiple_of(base + off, g), run, acc, jnp.concatenate([mask, mask], axis=1), off)

    def block(n, carry):
        run, acc = carry
        start = (i - 1 - n) * bq
        for d in range(per):
            run, acc = group(pl.multiple_of(start + (per - 1 - d) * g, g), run, acc, None, 0)
        return run, acc

    run, acc = lax.fori_loop(0, i, block, (run, acc))
    o_ref[...] = acc.astype(o_ref.dtype)


def _sb_attn(sb, batch):
    t = sb.shape[0]
    s = t // batch
    bq = SB_QUERY_BLOCK
    nq = s // bq
    pairs = SB_HEADS // 2
    return pl.pallas_call(
        _sb_kernel,
        grid=(batch, pairs, nq),
        in_specs=[pl.BlockSpec((bq, LANES), lambda b, p, i: (b * nq + i, p)),
                  pl.BlockSpec((s, LANES), lambda b, p, i: (b, pairs + p)),
                  pl.BlockSpec((s, LANES), lambda b, p, i: (b, 2 * pairs + p))],
        out_specs=pl.BlockSpec((bq, LANES), lambda b, p, i: (b * nq + i, p)),
        out_shape=jax.ShapeDtypeStruct((t, SB_HEADS * SB_DIM), BF16),
        compiler_params=pltpu.CompilerParams(dimension_semantics=("arbitrary",) * 3,
                                             vmem_limit_bytes=VMEM_LIMIT),
        name="sb_attn",
    )(sb, sb, sb)


def _mla_kernel(q_ref, k_ref, v_ref, o_ref):
    bq = q_ref.shape[0]
    bk = bq
    i = pl.program_id(2)
    lane = lax.broadcasted_iota(jnp.int32, (1, LANES), 1)
    lo = lane < HALF
    q = q_ref[...]
    row = lax.broadcasted_iota(jnp.int32, (bq, bk), 0)
    col = lax.broadcasted_iota(jnp.int32, (bq, bk), 1)
    causal = col <= row

    def visit(j, carry, masked):
        stats, acc = carry
        start = pl.multiple_of(j * bk, bk)
        k_b = k_ref[pl.ds(start, bk), :]
        v_b = v_ref[pl.ds(start, bk), :]
        new, probs, alphas = [], [], []
        for hh in range(2):
            m, l = stats[hh]
            sl = slice(hh * LANES, (hh + 1) * LANES)
            s = _dot_nt(q[:, sl], k_b[:, sl])
            if masked:
                s = jnp.where(causal, s, -jnp.inf)
            m_new = jnp.maximum(m, jnp.max(s, axis=-1, keepdims=True))
            alpha = jnp.exp2(m - m_new)
            p = jnp.exp2(s - m_new)
            new.append((m_new, alpha * l + jnp.sum(p, axis=-1, keepdims=True)))
            probs.append(p.astype(BF16))
            alphas.append(alpha)
        pv = _dot(jnp.concatenate(probs, axis=1), _head_rows(v_b, lo))
        return tuple(new), jnp.where(lo, alphas[0], alphas[1]) * acc + pv

    init = (tuple((jnp.full((bq, 1), -jnp.inf, F32), jnp.zeros((bq, 1), F32)) for _ in range(2)),
            jnp.zeros((bq, LANES), F32))
    carry = visit(i, init, True)
    stats, acc = lax.fori_loop(0, i, lambda n, cr: visit(n, cr, False), carry)
    o_ref[...] = (acc / jnp.where(lo, stats[0][1], stats[1][1])).astype(o_ref.dtype)


def _mla_attn(q, k, v, batch):
    t = q.shape[0]
    s = t // batch
    bq = MLA_BLOCK
    nq = s // bq
    pairs = MLA_HEADS // 2
    return pl.pallas_call(
        _mla_kernel,
        grid=(batch, pairs, nq),
        in_specs=[pl.BlockSpec((bq, 2 * LANES), lambda b, p, i: (b * nq + i, p)),
                  pl.BlockSpec((s, 2 * LANES), lambda b, p, i: (b, p)),
                  pl.BlockSpec((s, LANES), lambda b, p, i: (b, p))],
        out_specs=pl.BlockSpec((bq, LANES), lambda b, p, i: (b * nq + i, p)),
        out_shape=jax.ShapeDtypeStruct((t, MLA_HEADS * MLA_V), BF16),
        compiler_params=pltpu.CompilerParams(dimension_semantics=("arbitrary",) * 3,
                                             vmem_limit_bytes=VMEM_LIMIT),
        name="mla_attn",
    )(q, k, v)


def _post_kernel(h_ref, og_ref, os_ref, om_ref, w_o_ref, g1_ref, b1_ref, w_in_ref, w_out_ref,
                 g2_ref, b2_ref, p_ref, w_g_ref, w_p_ref, out_ref):
    n_g = GDN_HEADS * GDN_DV
    n_s = SB_HEADS * SB_DIM
    x = h_ref[...]
    mix = (_dot(og_ref[...], w_o_ref[:n_g, :]) + _dot(os_ref[...], w_o_ref[n_g:n_g + n_s, :])
           + _dot(om_ref[...], w_o_ref[n_g + n_s:, :]))
    h1 = _layer_norm(DEEPNORM_ALPHA * x + mix, g1_ref[...], b1_ref[...])
    y = _swiglu(h1.astype(BF16), w_in_ref, w_out_ref)
    h2 = _layer_norm(DEEPNORM_ALPHA * h1 + 0.5 * y, g2_ref[...], b2_ref[...])
    gate = _sigmoid(_dot(h2.astype(BF16), w_g_ref[...]))
    out_ref[...] = h2 + gate * _dot(p_ref[...].astype(BF16), w_p_ref[...])


def _post(h, o_gdn, o_sb, o_mla, w_o, g1, b1, w_in, w_out, g2, b2, p, w_g, w_p):
    t = h.shape[0]
    tm = TOKEN_TILE
    row = lambda n: pl.BlockSpec((tm, n), lambda i: (i, 0))
    return pl.pallas_call(
        _post_kernel,
        grid=(t // tm,),
        in_specs=[row(D_MODEL), row(o_gdn.shape[1]), row(o_sb.shape[1]), row(o_mla.shape[1]),
                  _resident(w_o.shape), _resident(g1.shape), _resident(b1.shape),
                  _resident(w_in.shape), _resident(w_out.shape), _resident(g2.shape), _resident(b2.shape),
                  row(PLE_DIM), _resident(w_g.shape), _resident(w_p.shape)],
        out_specs=row(D_MODEL),
        out_shape=jax.ShapeDtypeStruct((t, D_MODEL), F32),
        compiler_params=pltpu.CompilerParams(dimension_semantics=("arbitrary",),
                                             vmem_limit_bytes=VMEM_LIMIT),
        name="post",
    )(h, o_gdn, o_sb, o_mla, w_o, g1, b1, w_in, w_out, g2, b2, p, w_g, w_p)


def _mix_columns():
    hd = GDN_HEADS * GDN_DK
    off_q, off_k, off_v, off_z = 0, hd, 2 * hd, 3 * hd
    off_a = 4 * hd
    off_b = off_a + GDN_HEADS
    off_sb = off_b + GDN_HEADS
    off_mq = off_sb + N_SB
    off_ckv = off_mq + MLA_Q_RANK
    off_kr = off_ckv + MLA_KV_RANK
    cols = []
    for h in range(GDN_HEADS):
        cols += list(range(off_q + h * GDN_DK, off_q + (h + 1) * GDN_DK))
        cols += list(range(off_k + h * GDN_DK, off_k + (h + 1) * GDN_DK))
    for h in range(GDN_HEADS):
        cols += list(range(off_v + h * GDN_DV, off_v + (h + 1) * GDN_DV))
        cols += list(range(off_z + h * GDN_DV, off_z + (h + 1) * GDN_DV))
    cols += list(range(off_sb, off_sb + N_SB))
    cols += list(range(off_mq, off_mq + MLA_Q_RANK))
    cols += list(range(off_ckv, off_ckv + MLA_KV_RANK))
    half = MLA_ROPE // 2
    pad = LANES - MLA_NOPE - MLA_ROPE
    m1 = (list(range(off_a, off_a + GDN_HEADS)) + list(range(off_b, off_b + GDN_HEADS))
          + [-1] * (MLA_NOPE - 2 * GDN_HEADS) + list(range(off_kr, off_kr + MLA_ROPE)) + [-1] * pad)
    m2 = ([-1] * MLA_NOPE + list(range(off_kr + half, off_kr + MLA_ROPE))
          + list(range(off_kr, off_kr + half)) + [-1] * pad)
    cols += m1 + m2
    assert len(cols) == N_MIX
    return np.asarray(cols, np.int32)


def _uq_columns():
    d = MLA_NOPE + MLA_ROPE
    half = MLA_ROPE // 2
    pad = LANES - d
    main, swap = [], []
    for h in range(MLA_HEADS):
        main += list(range(h * d, (h + 1) * d)) + [-1] * pad
        swap += ([-1] * MLA_NOPE + list(range(h * d + MLA_NOPE + half, (h + 1) * d))
                 + list(range(h * d + MLA_NOPE, h * d + MLA_NOPE + half)) + [-1] * pad)
    return np.asarray(main + swap, np.int32)


def _ukv_columns():
    d = MLA_NOPE + MLA_V
    keys, vals = [], []
    for h in range(MLA_HEADS):
        keys += list(range(h * d, h * d + MLA_NOPE)) + [-1] * (LANES - MLA_NOPE)
        vals += list(range(h * d + MLA_NOPE, (h + 1) * d))
    return np.asarray(keys + vals, np.int32)


def _take_columns(w, cols):
    picked = jnp.take(w, jnp.asarray(np.maximum(cols, 0)), axis=-1)
    return jnp.where(jnp.asarray(cols >= 0), picked, 0.0)


def _conv_columns(conv_w):
    hd = GDN_HEADS * GDN_DK
    qk, vz = [], []
    for h in range(GDN_HEADS):
        qk += list(range(h * GDN_DK, (h + 1) * GDN_DK)) + list(range(hd + h * GDN_DK, hd + (h + 1) * GDN_DK))
        vz += list(range(2 * hd + h * GDN_DV, 2 * hd + (h + 1) * GDN_DV)) + [-1] * GDN_DV
    return _take_columns(conv_w, np.asarray(qk, np.int32)), _take_columns(conv_w, np.asarray(vz, np.int32))


def _lane_row(vec, start):
    return jnp.zeros((1, LANES), F32).at[0, start:start + vec.shape[0]].set(vec.astype(F32))


def _rope_tables(positions):
    inv = 1.0 / (ROPE_BASE ** (jnp.arange(0, MLA_ROPE, 2, dtype=F32) / MLA_ROPE))
    ang = positions.astype(F32).reshape(-1, 1) * inv
    cos, sin = jnp.cos(ang), jnp.sin(ang)
    t = ang.shape[0]
    pad = LANES - MLA_NOPE - MLA_ROPE
    cos_t = jnp.concatenate([jnp.ones((t, MLA_NOPE), F32), cos, cos, jnp.zeros((t, pad), F32)], axis=-1)
    sin_t = jnp.concatenate([jnp.zeros((t, MLA_NOPE), F32), -sin, sin, jnp.zeros((t, pad), F32)], axis=-1)
    return jnp.concatenate([cos_t, sin_t], axis=-1)


def kernel(x, p, positions, ffa_w_in, ffa_w_out, mix_w_in, gdn_conv_w, gdn_a_log, gdn_dt_bias, gdn_norm_w, mla_q_norm_w, mla_kv_norm_w, mla_w_uq, mla_w_ukv, mix_w_o, ffb_w_in, ffb_w_out, ln_g, ln_b, ple_w_gate, ple_w_proj):
    batch, seq, d = x.shape
    t = batch * seq
    depth = p.shape[0]
    cs = _rope_tables(positions)
    mix_cols, uq_cols, ukv_cols = _mix_columns(), _uq_columns(), _ukv_columns()
    h = x.reshape(t, d)
    for i in range(depth):
        w_mix = _take_columns(mix_w_in[i], mix_cols).astype(BF16)
        cw_qk, cw_vz = _conv_columns(gdn_conv_w[i])
        row = lambda v: v.reshape(1, -1)
        h, qk, vz, sb, ml = _ffn_inproj(h, ffa_w_in[i].astype(BF16), ffa_w_out[i].astype(BF16),
                                        row(ln_g[i, 0]), row(ln_b[i, 0]), w_mix)
        o_gdn = _gdn(qk, vz, ml, cw_qk, cw_vz, _lane_row(gdn_a_log[i], A_LANE), _lane_row(gdn_dt_bias[i], A_LANE),
                     row(jnp.concatenate([gdn_norm_w[i], gdn_norm_w[i]])), batch)
        q_m, k_m, v_m = _mla_prep(ml, cs, row(mla_q_norm_w[i]), row(mla_kv_norm_w[i]),
                                  _take_columns(mla_w_uq[i], uq_cols).astype(BF16),
                                  _take_columns(mla_w_ukv[i], ukv_cols).astype(BF16))
        o_sb = _sb_attn(sb, batch)
        o_mla = _mla_attn(q_m, k_m, v_m, batch)
        h = _post(h, o_gdn, o_sb, o_mla, mix_w_o[i].astype(BF16), row(ln_g[i, 1]), row(ln_b[i, 1]),
                  ffb_w_in[i].astype(BF16), ffb_w_out[i].astype(BF16), row(ln_g[i, 2]), row(ln_b[i, 2]),
                  p[i].reshape(t, -1), ple_w_gate[i].astype(BF16), ple_w_proj[i].astype(BF16))
    return h.reshape(batch, seq, d)
```

```python
import functools

import jax
import jax.numpy as jnp
from jax import lax
from jax.experimental import pallas as pl
from jax.experimental.pallas import tpu as pltpu

F32 = jnp.float32
BF16 = jnp.bfloat16

DEPTH = 2
D_MODEL = 1024
PLE_DIM = 256
D_FF = 2816
LN_EPS = 1e-5
RMS_EPS = 1e-6
DEEPNORM_ALPHA = (2 * DEPTH) ** 0.25

GDN_HEADS = 8
GDN_DK = 64
GDN_DV = 64
GDN_CONV = 4
SB_HEADS = 4
SB_DIM = 64
MLA_HEADS = 4
MLA_NOPE = 64
MLA_ROPE = 32
MLA_V = 64
MLA_Q_RANK = 256
MLA_KV_RANK = 128
ROPE_BASE = 10000.0

LOG2E = 1.4426950408889634
LANES = 128
HALF = LANES // 2
VMEM_LIMIT = 56 * 1024 * 1024

GDN_CHUNK = 128
GDN_CHUNKS_PER_STEP = 4
GDN_GROUP = 2
TOKEN_TILE = 256
SB_QUERY_BLOCK = 1024
MLA_BLOCK = 512

N_HD = GDN_HEADS * GDN_DK
N_QKV = 3 * N_HD
N_Z = GDN_HEADS * GDN_DV
N_SB = 3 * SB_HEADS * SB_DIM
N_ML = MLA_Q_RANK + MLA_KV_RANK + 2 * LANES
A_LANE = 0
B_LANE = GDN_HEADS
N_MIX = N_QKV + N_Z + N_SB + N_ML
N_MLA_QK = MLA_HEADS * LANES
N_MLA_V = MLA_HEADS * MLA_V
ROPE_ROWS = 8


def _silu(x):
    return x / (1.0 + jnp.exp(-x))


def _sigmoid(x):
    return 1.0 / (1.0 + jnp.exp(-x))


def _softplus(x):
    return jnp.maximum(x, 0.0) + jnp.log1p(jnp.exp(-jnp.abs(x)))


def _layer_norm(r, g, b):
    mu = jnp.mean(r, axis=-1, keepdims=True)
    d = r - mu
    var = jnp.mean(d * d, axis=-1, keepdims=True)
    return d * lax.rsqrt(var + LN_EPS) * g + b


def _rms_norm(x, w):
    return x * lax.rsqrt(jnp.mean(x * x, axis=-1, keepdims=True) + RMS_EPS) * w


def _dot(a, b):
    return jnp.dot(a, b, preferred_element_type=F32)


def _dot_nt(a, b):
    return lax.dot_general(a, b, (((1,), (1,)), ((), ())), preferred_element_type=F32)


def _head_rows(x, lo):
    zero = jnp.zeros_like(x)
    return jnp.concatenate([jnp.where(lo, x, zero), jnp.where(lo, zero, x)], axis=0)


def _replace_tail(full, r0, tail):
    return tail if r0 == 0 else jnp.concatenate([full[:r0], tail], axis=0)


def _split2(x):
    hi = x.astype(BF16)
    lo = (x - hi.astype(F32)).astype(BF16)
    return hi, lo


def _split3(x):
    hi = x.astype(BF16)
    r = x - hi.astype(F32)
    mid = r.astype(BF16)
    lo = (r - mid.astype(F32)).astype(BF16)
    return hi, mid, lo


def _swiglu(xb, w_in_ref, w_out_ref):
    gu = _dot(xb, w_in_ref[...])
    act = _silu(gu[:, :D_FF]) * gu[:, D_FF:]
    return _dot(act.astype(BF16), w_out_ref[...])


def _whole(a):
    return pl.BlockSpec(a.shape, lambda *_: (0,) * a.ndim, pipeline_mode=pl.Buffered(1))


def _of_layer(a, layer, block=None, at=None):
    tail = tuple(a.shape[1:]) if block is None else tuple(block)
    idx = (0,) * len(tail) if at is None else tuple(at)
    return pl.BlockSpec((None,) + tail, lambda *_: (layer,) + idx, pipeline_mode=pl.Buffered(1))


def _params(n_axes):
    return pltpu.CompilerParams(dimension_semantics=("arbitrary",) * n_axes, vmem_limit_bytes=VMEM_LIMIT)


def _ffn_inproj_kernel(layer, h_ref, pos_ref, rope_ref, w_in_ref, w_out_ref, g_ref, b_ref, w_mix_ref,
                       qnw_ref, kvnw_ref, w_uq_ref, w_ukv_ref,
                       h_out, qkv_out, z_out, sb_out, m1_out, mq_out, mk_out, mv_out):
    x = h_ref[...]
    y = _swiglu(x.astype(BF16), w_in_ref, w_out_ref)
    hn = _layer_norm(DEEPNORM_ALPHA * x + 0.5 * y, g_ref[0:1, :], b_ref[0:1, :])
    h_out[...] = hn
    proj = _dot(hn.astype(BF16), w_mix_ref[...])
    o = 0
    qkv_out[...] = proj[:, o:o + N_QKV]
    o += N_QKV
    z_out[...] = proj[:, o:o + N_Z]
    o += N_Z
    n_q = SB_HEADS * SB_DIM
    sb_out[:, :n_q] = (proj[:, o:o + n_q] * (SB_DIM ** -0.5 * LOG2E)).astype(BF16)
    sb_out[:, n_q:] = proj[:, o + n_q:o + N_SB].astype(BF16)
    o += N_SB
    mq = proj[:, o:o + MLA_Q_RANK]
    o += MLA_Q_RANK
    ckv = proj[:, o:o + MLA_KV_RANK]
    o += MLA_KV_RANK
    m1 = proj[:, o:o + LANES]
    m2 = proj[:, o + LANES:o + 2 * LANES]
    m1_out[...] = m1

    ang = pos_ref[...] * rope_ref[0:1, :]
    cos = jnp.cos(ang) * rope_ref[1:2, :]
    sin = jnp.sin(ang) * rope_ref[2:3, :]
    lane = lax.broadcasted_iota(jnp.int32, (1, LANES), 1)
    rot = (lane >= MLA_NOPE) & (lane < MLA_NOPE + MLA_ROPE)

    qf = _dot(_rms_norm(mq, qnw_ref[layer:layer + 1, :]).astype(BF16), w_uq_ref[...])
    kv = _dot(_rms_norm(ckv, kvnw_ref[layer:layer + 1, :]).astype(BF16), w_ukv_ref[...])
    scale = (MLA_NOPE + MLA_ROPE) ** -0.5 * LOG2E
    k_rot = jnp.where(rot, m1 * cos, 0.0) + m2 * sin
    for h in range(MLA_HEADS):
        sl = slice(h * LANES, (h + 1) * LANES)
        q_h = qf[:, sl] * cos + qf[:, N_MLA_QK + h * LANES:N_MLA_QK + (h + 1) * LANES] * sin
        mq_out[:, sl] = (q_h * scale).astype(BF16)
        mk_out[:, sl] = (kv[:, sl] + k_rot).astype(BF16)
    mv_out[...] = kv[:, N_MLA_QK:].astype(BF16)


def _ffn_inproj(layer, h, pos, rope, w_in, w_out, ln_g, ln_b, w_mix, qnw, kvnw, w_uq, w_ukv):
    t = h.shape[0]
    tm = TOKEN_TILE
    row = lambda n: pl.BlockSpec((tm, n), lambda i: (i, 0))
    widths = (D_MODEL, N_QKV, N_Z, N_SB, LANES, N_MLA_QK, N_MLA_QK, N_MLA_V)
    dtypes = (F32, F32, F32, BF16, F32, BF16, BF16, BF16)
    return pl.pallas_call(
        functools.partial(_ffn_inproj_kernel, layer),
        grid=(t // tm,),
        in_specs=[row(D_MODEL), row(1), _whole(rope), _of_layer(w_in, layer), _of_layer(w_out, layer),
                  _of_layer(ln_g, layer), _of_layer(ln_b, layer), _of_layer(w_mix, layer),
                  _whole(qnw), _whole(kvnw), _of_layer(w_uq, layer), _of_layer(w_ukv, layer)],
        out_specs=[row(n) for n in widths],
        out_shape=[jax.ShapeDtypeStruct((t, n), dt) for n, dt in zip(widths, dtypes)],
        compiler_params=_params(1),
        name="ffn_inproj",
    )(h, pos, rope, w_in, w_out, ln_g, ln_b, w_mix, qnw, kvnw, w_uq, w_ukv)


def _unit_lower_inverses(l_strict, row, col):
    c = l_strict[0].shape[0]
    blk = lambda v, log2: jnp.right_shift(v, log2)
    pair = (blk(row, 1) == blk(col, 1)) & (row > col)
    eye = jnp.where(row == col, 1.0, 0.0)
    xs = [eye - jnp.where(pair, l, 0.0) for l in l_strict]
    log2 = 1
    while (2 << log2) <= c:
        off = (blk(row, log2 + 1) == blk(col, log2 + 1)) & (blk(row, log2) > blk(col, log2))
        xb = [x.astype(BF16) for x in xs]
        lx = [_dot(jnp.where(off, l, 0.0).astype(BF16), b).astype(BF16) for l, b in zip(l_strict, xb)]
        yield
        xs = [x - _dot(b, t) for x, b, t in zip(xs, xb, lx)]
        yield
        log2 += 1
    return xs


def _interleave(main, side):
    values, live = [None, None], [main, side]
    while any(g is not None for g in live):
        for n, g in enumerate(live):
            if g is not None:
                try:
                    next(g)
                except StopIteration as stop:
                    values[n], live[n] = stop.value, None
    return values


def _dot_exact(a, b, terms):
    split = {2: _split2, 3: _split3}[terms]
    if a.dtype == F32:
        parts = [_dot(piece, b) for piece in split(a)]
    else:
        parts = [_dot(a, piece) for piece in split(b)]
    return functools.reduce(lambda x, y: x + y, parts)


def _gdn_kernel(layer, qkv_ref, z_ref, m1_ref, cw_ref, alog_ref, dtb_ref, nw_ref, o_ref, buf, state_ref):
    c = GDN_CHUNK
    chunks = qkv_ref.shape[0] // c
    pairs = GDN_HEADS // 2
    first = pl.program_id(1) == 0

    @pl.when(first)
    def _():
        state_ref[...] = jnp.zeros(state_ref.shape, F32)

    lane = lax.broadcasted_iota(jnp.int32, (1, LANES), 1)
    lo = lane < HALF
    row = lax.broadcasted_iota(jnp.int32, (c, c), 0)
    col = lax.broadcasted_iota(jnp.int32, (c, c), 1)
    incl = row >= col
    strict = row > col
    tril = jnp.where(incl, 1.0, 0.0).astype(BF16)
    same_head = jnp.where(jnp.right_shift(row, 6) == jnp.right_shift(col, 6), 1.0, 0.0).astype(BF16)
    wide_row = lax.broadcasted_iota(jnp.int32, (LANES, GDN_HEADS * LANES), 0)
    wide_head = jnp.right_shift(lax.broadcasted_iota(jnp.int32, (LANES, GDN_HEADS * LANES), 1), 7)
    pick_b = jnp.where(wide_row == wide_head + B_LANE, 1.0, 0.0).astype(BF16)
    a_row = -jnp.exp(alog_ref[layer:layer + 1, :])
    dt_row = dtb_ref[layer:layer + 1, :]
    nw = nw_ref[layer:layer + 1, :]
    heads = range(GDN_HEADS)
    hs = [slice(h * LANES, (h + 1) * LANES) for h in heads]
    ps = [slice(p * LANES, (p + 1) * LANES) for p in range(pairs)]

    @pl.when(first)
    def _():
        buf[...] = jnp.zeros(buf.shape, F32)


    def prepare(j):
        rows = slice(j * c, (j + 1) * c)

        def conv_silu(group):
            sl = slice(group * LANES, (group + 1) * LANES)
            x = qkv_ref[rows, sl]
            before = buf[:, sl] if j == 0 else qkv_ref[j * c - 8:j * c, sl]
            xx = jnp.concatenate([before, x], axis=0)
            y = x * cw_ref[GDN_CONV - 1:GDN_CONV, sl]
            for tap in range(GDN_CONV - 1):
                y = y + pltpu.roll(xx, GDN_CONV - 1 - tap, axis=0)[8:] * cw_ref[tap:tap + 1, sl]
            return _silu(y)

        m1 = m1_ref[rows, :]
        gc = _dot_exact(tril, a_row * _softplus(m1 + dt_row), 3)
        gc_rows = gc.T
        beta_wide = _dot_exact(_sigmoid(m1), pick_b, 2)
        yield
        out = []
        for p in range(pairs):
            xq, xk, xv = conv_silu(p), conv_silu(pairs + p), conv_silu(2 * pairs + p)
            yield
            qn = xq * (lax.rsqrt(_dot_exact(xq * xq, same_head, 2) + RMS_EPS) * (GDN_DK ** -0.5))
            kn = xk * lax.rsqrt(_dot_exact(xk * xk, same_head, 2) + RMS_EPS)
            yield
            for h in (2 * p, 2 * p + 1):
                mine = lo if h % 2 == 0 else jnp.logical_not(lo)
                gcb = jnp.broadcast_to(gc[:, A_LANE + h:A_LANE + h + 1], (c, LANES))
                gc_row = gc_rows[A_LANE + h:A_LANE + h + 1, :]
                out.append(dict(
                    q=jnp.where(mine, qn, 0.0), k=jnp.where(mine, kn, 0.0), v=jnp.where(mine, xv, 0.0),
                    gcb=gcb, beta=beta_wide[:, hs[h]],
                    decay=jnp.where(incl, jnp.exp(jnp.where(incl, gcb - gc_row, 0.0)), 0.0), e=jnp.exp(gcb)))
                yield
        return out

    def solve(pre):
        gram = [_dot_nt(jnp.concatenate([t["q"], t["k"]], axis=0).astype(BF16), t["k"].astype(BF16))
                for t in pre]
        yield
        l_strict = [jnp.where(strict, g[c:] * t["decay"] * t["beta"], 0.0) for g, t in zip(gram, pre)]
        t_inv = yield from _unit_lower_inverses(l_strict, row, col)
        out = []
        for g, t, ti in zip(gram, pre, t_inv):
            uw = _dot(ti.astype(BF16), jnp.concatenate([t["v"] * t["beta"], t["k"] * (t["beta"] * t["e"])],
                                                       axis=1).astype(BF16))
            g_end = t["gcb"][c - 1:c, :]
            out.append(dict(
                u=uw[:, :LANES], wq=jnp.concatenate([uw[:, LANES:], t["q"] * t["e"]], axis=0).astype(BF16),
                qk_kd=jnp.concatenate([g[:c] * t["decay"], (t["k"] * jnp.exp(g_end - t["gcb"])).T],
                                      axis=0).astype(BF16),
                keep=jnp.exp(g_end)))
        yield
        return out

    def advance(j, sol, state):
        ws = [_dot(s["wq"], st.astype(BF16)) for s, st in zip(sol, state)]
        yield
        fin = [_dot(s["qk_kd"], (s["u"] - w[:c]).astype(BF16)) for s, w in zip(sol, ws)]
        yield
        new_state = [st * s["keep"] + f[c:] for st, s, f in zip(state, sol, fin)]
        rows = slice(j * c, (j + 1) * c)
        for p in range(pairs):
            o_pair = (ws[2 * p][c:] + fin[2 * p][:c]) + (ws[2 * p + 1][c:] + fin[2 * p + 1][:c])
            ms = _dot_exact(o_pair * o_pair, same_head, 2) * (1.0 / GDN_DV)
            o_ref[rows, ps[p]] = (o_pair * lax.rsqrt(ms + RMS_EPS) * nw * _silu(z_ref[rows, ps[p]])).astype(o_ref.dtype)
            yield
        return new_state

    def prepare_group(js):
        out = []
        for j in js:
            out += yield from prepare(j)
        return out

    def group_matmuls(js, pre, state):
        sol = yield from solve(pre)
        for n, j in enumerate(js):
            state = yield from advance(j, sol[n * GDN_HEADS:(n + 1) * GDN_HEADS], state)
        return state

    def nothing():
        return None
        yield

    groups = [list(range(g0, min(g0 + GDN_GROUP, chunks))) for g0 in range(0, chunks, GDN_GROUP)]
    state = [state_ref[h] for h in heads]
    _, pre = _interleave(nothing(), prepare_group(groups[0]))
    for n, js in enumerate(groups):
        upcoming = prepare_group(groups[n + 1]) if n + 1 < len(groups) else nothing()
        state, pre = _interleave(group_matmuls(js, pre, state), upcoming)
    for h in heads:
        state_ref[h] = state[h]
    buf[...] = qkv_ref[chunks * c - 8:chunks * c, :]


def _gdn(layer, qkv, z, m1, conv_w, alog, dtb, nw, batch):
    t = qkv.shape[0]
    c = GDN_CHUNK * GDN_CHUNKS_PER_STEP
    nc = t // batch // c
    blk = lambda n: pl.BlockSpec((c, n), lambda b, i: (b * nc + i, 0))
    return pl.pallas_call(
        functools.partial(_gdn_kernel, layer),
        grid=(batch, nc),
        in_specs=[blk(N_QKV), blk(N_Z), blk(LANES), _of_layer(conv_w, layer),
                  _whole(alog), _whole(dtb), _whole(nw)],
        out_specs=blk(GDN_HEADS * GDN_DV),
        out_shape=jax.ShapeDtypeStruct((t, GDN_HEADS * GDN_DV), BF16),
        scratch_shapes=[pltpu.VMEM((8, N_QKV), F32), pltpu.VMEM((GDN_HEADS, LANES, LANES), F32)],
        compiler_params=_params(2),
        name="gdn",
    )(qkv, z, m1, conv_w, alog, dtb, nw)


def _sb_kernel(q_ref, k_ref, v_ref, o_ref):
    bq = q_ref.shape[0]
    g = LANES
    per = bq // g
    i = pl.program_id(2)
    lane = lax.broadcasted_iota(jnp.int32, (1, LANES), 1)
    lo = lane < HALF
    q = q_ref[...]
    row = lax.broadcasted_iota(jnp.int32, (bq, g), 0)
    col = lax.broadcasted_iota(jnp.int32, (bq, g), 1)
    jr = lax.broadcasted_iota(jnp.int32, (g, g), 0)
    jc = lax.broadcasted_iota(jnp.int32, (g, g), 1)
    later = jnp.where(jr > jc, 1.0, 0.0).astype(BF16)
    suffix_total = jnp.concatenate([later, jnp.ones((g, g), BF16)], axis=1)
    suffix_total = jnp.concatenate([suffix_total, suffix_total], axis=0)

    def group(start, run, acc, mask, r0):
        k_g = k_ref[pl.ds(start, g), :]
        v_g = v_ref[pl.ds(start, g), :]
        z = _dot_nt(q[r0:], _head_rows(k_g, lo))
        nl = jnp.maximum(z, 0.0) + jnp.log(1.0 + jnp.exp2(-jnp.abs(z))) * LOG2E
        if mask is not None:
            nl = jnp.where(mask, nl, 0.0)
        wts, new_run = [], []
        for hh in range(2):
            sl = slice(hh * g, (hh + 1) * g)
            hi, low = _split2(nl[:, sl])
            st = _dot(jnp.concatenate([hi, low], axis=1), suffix_total)
            wts.append(jnp.exp2(z[:, sl] - nl[:, sl] - (run[hh][r0:] + st[:, :g])))
            new_run.append(_replace_tail(run[hh], r0, run[hh][r0:] + st[:, g:]))
        wts = jnp.concatenate(wts, axis=1)
        if mask is not None:
            wts = jnp.where(mask, wts, 0.0)
        pv = _dot(wts.astype(BF16), _head_rows(v_g, lo))
        return tuple(new_run), _replace_tail(acc, r0, acc[r0:] + pv)

    run = (jnp.zeros((bq, g), F32), jnp.zeros((bq, g), F32))
    acc = jnp.zeros((bq, LANES), F32)
    base = i * bq
    for d in range(per):
        off = (per - 1 - d) * g
        mask = ((col + off) < row)[off:]
        run, acc = group(pl.multiple_of(base + off, g), run, acc, jnp.concatenate([mask, mask], axis=1), off)

    def block(n, carry):
        run, acc = carry
        start = (i - 1 - n) * bq
        for d in range(per):
            run, acc = group(pl.multiple_of(start + (per - 1 - d) * g, g), run, acc, None, 0)
        return run, acc

    run, acc = lax.fori_loop(0, i, block, (run, acc))
    o_ref[...] = acc.astype(o_ref.dtype)


def _sb_attn(sb, batch):
    t = sb.shape[0]
    s = t // batch
    bq = SB_QUERY_BLOCK
    nq = s // bq
    pairs = SB_HEADS // 2
    return pl.pallas_call(
        _sb_kernel,
        grid=(batch, pairs, nq),
        in_specs=[pl.BlockSpec((bq, LANES), lambda b, p, i: (b * nq + i, p)),
                  pl.BlockSpec((s, LANES), lambda b, p, i: (b, pairs + p)),
                  pl.BlockSpec((s, LANES), lambda b, p, i: (b, 2 * pairs + p))],
        out_specs=pl.BlockSpec((bq, LANES), lambda b, p, i: (b * nq + i, p)),
        out_shape=jax.ShapeDtypeStruct((t, SB_HEADS * SB_DIM), BF16),
        compiler_params=_params(3),
        name="sb_attn",
    )(sb, sb, sb)


def _mla_kernel(q_ref, k_ref, v_ref, o_ref):
    bq = q_ref.shape[0]
    bk = bq
    i = pl.program_id(2)
    lane = lax.broadcasted_iota(jnp.int32, (1, LANES), 1)
    lo = lane < HALF
    q = q_ref[...]
    row = lax.broadcasted_iota(jnp.int32, (bq, bk), 0)
    col = lax.broadcasted_iota(jnp.int32, (bq, bk), 1)
    causal = col <= row

    def visit(j, carry, masked):
        stats, acc = carry
        start = pl.multiple_of(j * bk, bk)
        k_b = k_ref[pl.ds(start, bk), :]
        v_b = v_ref[pl.ds(start, bk), :]
        new, probs, alphas = [], [], []
        for hh in range(2):
            m, l = stats[hh]
            sl = slice(hh * LANES, (hh + 1) * LANES)
            s = _dot_nt(q[:, sl], k_b[:, sl])
            if masked:
                s = jnp.where(causal, s, -jnp.inf)
            m_new = jnp.maximum(m, jnp.max(s, axis=-1, keepdims=True))
            alpha = jnp.exp2(m - m_new)
            p = jnp.exp2(s - m_new)
            new.append((m_new, alpha * l + jnp.sum(p, axis=-1, keepdims=True)))
            probs.append(p.astype(BF16))
            alphas.append(alpha)
        pv = _dot(jnp.concatenate(probs, axis=1), _head_rows(v_b, lo))
        return tuple(new), jnp.where(lo, alphas[0], alphas[1]) * acc + pv

    init = (tuple((jnp.full((bq, 1), -jnp.inf, F32), jnp.zeros((bq, 1), F32)) for _ in range(2)),
            jnp.zeros((bq, LANES), F32))
    carry = visit(i, init, True)
    stats, acc = lax.fori_loop(0, i, lambda n, cr: visit(n, cr, False), carry)
    o_ref[...] = (acc / jnp.where(lo, stats[0][1], stats[1][1])).astype(o_ref.dtype)


def _mla_attn(q, k, v, batch):
    t = q.shape[0]
    s = t // batch
    bq = MLA_BLOCK
    nq = s // bq
    pairs = MLA_HEADS // 2
    return pl.pallas_call(
        _mla_kernel,
        grid=(batch, pairs, nq),
        in_specs=[pl.BlockSpec((bq, 2 * LANES), lambda b, p, i: (b * nq + i, p)),
                  pl.BlockSpec((s, 2 * LANES), lambda b, p, i: (b, p)),
                  pl.BlockSpec((s, LANES), lambda b, p, i: (b, p))],
        out_specs=pl.BlockSpec((bq, LANES), lambda b, p, i: (b * nq + i, p)),
        out_shape=jax.ShapeDtypeStruct((t, MLA_HEADS * MLA_V), BF16),
        compiler_params=_params(3),
        name="mla_attn",
    )(q, k, v)


def _post_kernel(h_ref, og_ref, os_ref, om_ref, w_o_ref, g_ref, b_ref, w_in_ref, w_out_ref,
                 p_ref, w_g_ref, w_p_ref, out_ref):
    n_g = GDN_HEADS * GDN_DV
    n_s = SB_HEADS * SB_DIM
    x = h_ref[...]
    mix = (_dot(og_ref[...], w_o_ref[:n_g, :]) + _dot(os_ref[...], w_o_ref[n_g:n_g + n_s, :])
           + _dot(om_ref[...], w_o_ref[n_g + n_s:, :]))
    h1 = _layer_norm(DEEPNORM_ALPHA * x + mix, g_ref[1:2, :], b_ref[1:2, :])
    y = _swiglu(h1.astype(BF16), w_in_ref, w_out_ref)
    h2 = _layer_norm(DEEPNORM_ALPHA * h1 + 0.5 * y, g_ref[2:3, :], b_ref[2:3, :])
    gate = _sigmoid(_dot(h2.astype(BF16), w_g_ref[...]))
    out_ref[...] = h2 + gate * _dot(p_ref[...].astype(BF16), w_p_ref[...])


def _post(layer, h, o_gdn, o_sb, o_mla, w_o, ln_g, ln_b, w_in, w_out, p, w_g, w_p):
    t = h.shape[0]
    tm = TOKEN_TILE
    row = lambda n: pl.BlockSpec((tm, n), lambda i: (i, 0))
    return pl.pallas_call(
        _post_kernel,
        grid=(t // tm,),
        in_specs=[row(D_MODEL), row(o_gdn.shape[1]), row(o_sb.shape[1]), row(o_mla.shape[1]),
                  _of_layer(w_o, layer), _of_layer(ln_g, layer), _of_layer(ln_b, layer),
                  _of_layer(w_in, layer), _of_layer(w_out, layer),
                  pl.BlockSpec((None, tm, PLE_DIM), lambda i: (layer, i, 0)),
                  _of_layer(w_g, layer), _of_layer(w_p, layer)],
        out_specs=row(D_MODEL),
        out_shape=jax.ShapeDtypeStruct((t, D_MODEL), F32),
        compiler_params=_params(1),
        name="post",
    )(h, o_gdn, o_sb, o_mla, w_o, ln_g, ln_b, w_in, w_out, p, w_g, w_p)


def _zeros_like_cols(w, n):
    return jnp.zeros(w.shape[:-1] + (n,), w.dtype)


def _regroup_mix(w):
    edges = [N_QKV + N_Z, GDN_HEADS, GDN_HEADS, N_SB + MLA_Q_RANK + MLA_KV_RANK, MLA_ROPE]
    parts, o = [], 0
    for n in edges:
        parts.append(w[..., o:o + n])
        o += n
    gdn, ga, gb, rest, kr = parts
    half = MLA_ROPE // 2
    pad = LANES - MLA_NOPE - MLA_ROPE
    m1 = [ga, gb, _zeros_like_cols(w, MLA_NOPE - 2 * GDN_HEADS), kr, _zeros_like_cols(w, pad)]
    m2 = [_zeros_like_cols(w, MLA_NOPE), kr[..., half:], kr[..., :half], _zeros_like_cols(w, pad)]
    return jnp.concatenate([gdn, rest] + m1 + m2, axis=-1)


def _regroup_uq(w):
    d = MLA_NOPE + MLA_ROPE
    half = MLA_ROPE // 2
    pad = _zeros_like_cols(w, LANES - d)
    main, swap = [], []
    for h in range(MLA_HEADS):
        main += [w[..., h * d:(h + 1) * d], pad]
        swap += [_zeros_like_cols(w, MLA_NOPE), w[..., h * d + MLA_NOPE + half:(h + 1) * d],
                 w[..., h * d + MLA_NOPE:h * d + MLA_NOPE + half], pad]
    return jnp.concatenate(main + swap, axis=-1)


def _regroup_ukv(w):
    d = MLA_NOPE + MLA_V
    keys, vals = [], []
    for h in range(MLA_HEADS):
        keys += [w[..., h * d:h * d + MLA_NOPE], _zeros_like_cols(w, LANES - MLA_NOPE)]
        vals.append(w[..., h * d + MLA_NOPE:(h + 1) * d])
    return jnp.concatenate(keys + vals, axis=-1)


def _lane_rows(vals, start):
    return jnp.pad(vals.astype(F32), ((0, 0), (start, LANES - start - vals.shape[1])))


def _rope_rows():
    inv = 1.0 / (ROPE_BASE ** (jnp.arange(0, MLA_ROPE, 2, dtype=F32) / MLA_ROPE))
    half = MLA_ROPE // 2
    pad = LANES - MLA_NOPE - MLA_ROPE
    z = lambda n: jnp.zeros((n,), F32)
    one = lambda n: jnp.ones((n,), F32)
    freq = jnp.concatenate([z(MLA_NOPE), inv, inv, z(pad)])
    cos_mask = jnp.concatenate([one(MLA_NOPE + MLA_ROPE), z(pad)])
    sin_sign = jnp.concatenate([z(MLA_NOPE), -one(half), one(half), z(pad)])
    rows = jnp.stack([freq, cos_mask, sin_sign])
    return jnp.pad(rows, ((0, ROPE_ROWS - rows.shape[0]), (0, 0)))


def kernel(x, p, positions, ffa_w_in, ffa_w_out, mix_w_in, gdn_conv_w, gdn_a_log, gdn_dt_bias, gdn_norm_w, mla_q_norm_w, mla_kv_norm_w, mla_w_uq, mla_w_ukv, mix_w_o, ffb_w_in, ffb_w_out, ln_g, ln_b, ple_w_gate, ple_w_proj):
    batch, seq, d = x.shape
    t = batch * seq
    depth = p.shape[0]
    hd = GDN_HEADS * GDN_DK

    bf = lambda w: w.astype(BF16)
    w_mix, w_uq, w_ukv = bf(_regroup_mix(mix_w_in)), bf(_regroup_uq(mla_w_uq)), bf(_regroup_ukv(mla_w_ukv))
    ffa_in, ffa_out, ffb_in, ffb_out = bf(ffa_w_in), bf(ffa_w_out), bf(ffb_w_in), bf(ffb_w_out)
    w_o, w_g, w_p = bf(mix_w_o), bf(ple_w_gate), bf(ple_w_proj)
    alog, dtb = _lane_rows(gdn_a_log, A_LANE), _lane_rows(gdn_dt_bias, A_LANE)
    nw = jnp.concatenate([gdn_norm_w, gdn_norm_w], axis=-1).astype(F32)
    rope = _rope_rows()
    pos = positions.reshape(t, 1).astype(F32)
    p_rows = p.reshape(depth, t, p.shape[-1])

    h = x.reshape(t, d)
    for i in range(depth):
        h, qkv, z, sb, m1, q_m, k_m, v_m = _ffn_inproj(i, h, pos, rope, ffa_in, ffa_out, ln_g, ln_b, w_mix,
                                                       mla_q_norm_w, mla_kv_norm_w, w_uq, w_ukv)
        o_gdn = _gdn(i, qkv, z, m1, gdn_conv_w, alog, dtb, nw, batch)
        o_sb = _sb_attn(sb, batch)
        o_mla = _mla_attn(q_m, k_m, v_m, batch)
        h = _post(i, h, o_gdn, o_sb, o_mla, w_o, ln_g, ln_b, ffb_in, ffb_out, p_rows, w_g, w_p)
    return h.reshape(batch, seq, d)
```

```python
import functools

import jax
import jax.numpy as jnp
from jax import lax
from jax.experimental import pallas as pl
from jax.experimental.pallas import tpu as pltpu

F32 = jnp.float32
BF16 = jnp.bfloat16

DEPTH = 2
D_MODEL = 1024
PLE_DIM = 256
D_FF = 2816
LN_EPS = 1e-5
RMS_EPS = 1e-6
DEEPNORM_ALPHA = (2 * DEPTH) ** 0.25

GDN_HEADS = 8
GDN_DK = 64
GDN_DV = 64
GDN_CONV = 4
SB_HEADS = 4
SB_DIM = 64
MLA_HEADS = 4
MLA_NOPE = 64
MLA_ROPE = 32
MLA_V = 64
MLA_Q_RANK = 256
MLA_KV_RANK = 128
ROPE_BASE = 10000.0

LOG2E = 1.4426950408889634
LANES = 128
HALF = LANES // 2
VMEM_LIMIT = 56 * 1024 * 1024

GDN_CHUNK = 128
GDN_CHUNKS_PER_STEP = 4
GDN_GROUP = 2
TOKEN_TILE = 512
SB_QUERY_BLOCK = 1024
MLA_BLOCK = 1024
MLA_KEY_BLOCK = 1024

N_HD = GDN_HEADS * GDN_DK
N_QKV = 3 * N_HD
N_Z = GDN_HEADS * GDN_DV
N_SB = 3 * SB_HEADS * SB_DIM
N_ML = MLA_Q_RANK + MLA_KV_RANK + 2 * LANES
A_LANE = 0
B_LANE = GDN_HEADS
N_MIX = N_QKV + N_Z + N_SB + N_ML
N_MLA_QK = MLA_HEADS * LANES
N_MLA_V = MLA_HEADS * MLA_V
ROPE_ROWS = 8


def _silu(x):
    return x / (1.0 + jnp.exp(-x))


def _sigmoid(x):
    return 1.0 / (1.0 + jnp.exp(-x))


def _softplus(x):
    return jnp.maximum(x, 0.0) + jnp.log1p(jnp.exp(-jnp.abs(x)))


def _layer_norm(r, g, b):
    mu = jnp.mean(r, axis=-1, keepdims=True)
    d = r - mu
    var = jnp.mean(d * d, axis=-1, keepdims=True)
    return d * lax.rsqrt(var + LN_EPS) * g + b


def _rms_norm(x, w):
    return x * lax.rsqrt(jnp.mean(x * x, axis=-1, keepdims=True) + RMS_EPS) * w


def _dot(a, b):
    return jnp.dot(a, b, preferred_element_type=F32)


def _dot_nt(a, b):
    return lax.dot_general(a, b, (((1,), (1,)), ((), ())), preferred_element_type=F32)


def _head_rows(x, lo):
    zero = jnp.zeros_like(x)
    return jnp.concatenate([jnp.where(lo, x, zero), jnp.where(lo, zero, x)], axis=0)


def _replace_tail(full, r0, tail):
    return tail if r0 == 0 else jnp.concatenate([full[:r0], tail], axis=0)


def _split2(x):
    hi = x.astype(BF16)
    lo = (x - hi.astype(F32)).astype(BF16)
    return hi, lo


def _split3(x):
    hi = x.astype(BF16)
    r = x - hi.astype(F32)
    mid = r.astype(BF16)
    lo = (r - mid.astype(F32)).astype(BF16)
    return hi, mid, lo


def _swiglu(xb, w_in_ref, w_out_ref):
    gu = _dot(xb, w_in_ref[...])
    act = _silu(gu[:, :D_FF]) * gu[:, D_FF:]
    return _dot(act.astype(BF16), w_out_ref[...])


def _whole(a):
    return pl.BlockSpec(a.shape, lambda *_: (0,) * a.ndim, pipeline_mode=pl.Buffered(1))


def _of_layer(a, layer, block=None, at=None):
    tail = tuple(a.shape[1:]) if block is None else tuple(block)
    idx = (0,) * len(tail) if at is None else tuple(at)
    return pl.BlockSpec((None,) + tail, lambda *_: (layer,) + idx, pipeline_mode=pl.Buffered(1))


def _params(n_axes):
    return pltpu.CompilerParams(dimension_semantics=("arbitrary",) * n_axes, vmem_limit_bytes=VMEM_LIMIT)


def _ffn_inproj_kernel(layer, h_ref, pos_ref, rope_ref, w_in_ref, w_out_ref, g_ref, b_ref,
                       w_gdn_ref, w_att_ref, w_m_ref, qnw_ref, kvnw_ref, w_uq_ref, w_ukv_ref,
                       h_out, qkv_out, z_out, sb_out, m1_out, mq_out, mk_out, mv_out):
    x = h_ref[...]
    y = _swiglu(x.astype(BF16), w_in_ref, w_out_ref)
    hn = _layer_norm(DEEPNORM_ALPHA * x + 0.5 * y, g_ref[0:1, :], b_ref[0:1, :])
    h_out[...] = hn
    hb = hn.astype(BF16)
    gdn = _dot(hb, w_gdn_ref[...])
    qkv_out[...] = gdn[:, :N_QKV]
    z_out[...] = gdn[:, N_QKV:]
    att = _dot(hb, w_att_ref[...])
    n_q = SB_HEADS * SB_DIM
    sb_out[:, :n_q] = (att[:, :n_q] * (SB_DIM ** -0.5 * LOG2E)).astype(BF16)
    sb_out[:, n_q:] = att[:, n_q:N_SB].astype(BF16)
    mq = att[:, N_SB:N_SB + MLA_Q_RANK]
    ckv = att[:, N_SB + MLA_Q_RANK:]
    m12 = _dot(hb, w_m_ref[...])
    m1 = m12[:, :LANES]
    m2 = m12[:, LANES:]
    m1_out[...] = m1

    ang = pos_ref[...] * rope_ref[0:1, :]
    cos = jnp.cos(ang) * rope_ref[1:2, :]
    sin = jnp.sin(ang) * rope_ref[2:3, :]
    lane = lax.broadcasted_iota(jnp.int32, (1, LANES), 1)
    rot = (lane >= MLA_NOPE) & (lane < MLA_NOPE + MLA_ROPE)

    qf = _dot(_rms_norm(mq, qnw_ref[layer:layer + 1, :]).astype(BF16), w_uq_ref[...])
    kv = _dot(_rms_norm(ckv, kvnw_ref[layer:layer + 1, :]).astype(BF16), w_ukv_ref[...])
    scale = (MLA_NOPE + MLA_ROPE) ** -0.5 * LOG2E
    k_rot = jnp.where(rot, m1 * cos, 0.0) + m2 * sin
    for h in range(MLA_HEADS):
        sl = slice(h * LANES, (h + 1) * LANES)
        q_h = qf[:, sl] * cos + qf[:, N_MLA_QK + h * LANES:N_MLA_QK + (h + 1) * LANES] * sin
        mq_out[:, sl] = (q_h * scale).astype(BF16)
        mk_out[:, sl] = (kv[:, sl] + k_rot).astype(BF16)
    mv_out[...] = kv[:, N_MLA_QK:].astype(BF16)


def _ffn_inproj(layer, h, pos, rope, w_in, w_out, ln_g, ln_b, w_mix, qnw, kvnw, w_uq, w_ukv):
    w_gdn, w_att, w_m = w_mix
    t = h.shape[0]
    tm = TOKEN_TILE
    row = lambda n: pl.BlockSpec((tm, n), lambda i: (i, 0))
    widths = (D_MODEL, N_QKV, N_Z, N_SB, LANES, N_MLA_QK, N_MLA_QK, N_MLA_V)
    dtypes = (F32, F32, F32, BF16, F32, BF16, BF16, BF16)
    return pl.pallas_call(
        functools.partial(_ffn_inproj_kernel, layer),
        grid=(t // tm,),
        in_specs=[row(D_MODEL), row(1), _whole(rope), _of_layer(w_in, layer), _of_layer(w_out, layer),
                  _of_layer(ln_g, layer), _of_layer(ln_b, layer),
                  _of_layer(w_gdn, layer), _of_layer(w_att, layer), _of_layer(w_m, layer),
                  _whole(qnw), _whole(kvnw), _of_layer(w_uq, layer), _of_layer(w_ukv, layer)],
        out_specs=[row(n) for n in widths],
        out_shape=[jax.ShapeDtypeStruct((t, n), dt) for n, dt in zip(widths, dtypes)],
        compiler_params=_params(1),
        name="ffn_inproj",
    )(h, pos, rope, w_in, w_out, ln_g, ln_b, w_gdn, w_att, w_m, qnw, kvnw, w_uq, w_ukv)


def _unit_lower_inverses(l_strict, row, col):
    c = l_strict[0].shape[0]
    blk = lambda v, log2: jnp.right_shift(v, log2)
    pair = (blk(row, 1) == blk(col, 1)) & (row > col)
    eye = jnp.where(row == col, 1.0, 0.0)
    xs = [eye - jnp.where(pair, l, 0.0) for l in l_strict]
    log2 = 1
    while (2 << log2) <= c:
        off = (blk(row, log2 + 1) == blk(col, log2 + 1)) & (blk(row, log2) > blk(col, log2))
        xb = [x.astype(BF16) for x in xs]
        lx = [_dot(jnp.where(off, l, 0.0).astype(BF16), b).astype(BF16) for l, b in zip(l_strict, xb)]
        yield
        xs = [x - _dot(b, t) for x, b, t in zip(xs, xb, lx)]
        yield
        log2 += 1
    return xs


def _interleave(main, side):
    values, live = [None, None], [main, side]
    while any(g is not None for g in live):
        for n, g in enumerate(live):
            if g is not None:
                try:
                    next(g)
                except StopIteration as stop:
                    values[n], live[n] = stop.value, None
    return values


def _dot_exact(a, b, terms):
    split = {2: _split2, 3: _split3}[terms]
    if a.dtype == F32:
        parts = [_dot(piece, b) for piece in split(a)]
    else:
        parts = [_dot(a, piece) for piece in split(b)]
    return functools.reduce(lambda x, y: x + y, parts)


def _gdn_kernel(layer, qkv_ref, z_ref, m1_ref, cw_ref, alog_ref, dtb_ref, nw_ref, o_ref, buf, state_ref):
    c = GDN_CHUNK
    chunks = qkv_ref.shape[0] // c
    pairs = GDN_HEADS // 2
    first = pl.program_id(1) == 0

    @pl.when(first)
    def _():
        state_ref[...] = jnp.zeros(state_ref.shape, F32)

    lane = lax.broadcasted_iota(jnp.int32, (1, LANES), 1)
    lo = lane < HALF
    row = lax.broadcasted_iota(jnp.int32, (c, c), 0)
    col = lax.broadcasted_iota(jnp.int32, (c, c), 1)
    incl = row >= col
    strict = row > col
    tril = jnp.where(incl, 1.0, 0.0).astype(BF16)
    same_head = jnp.where(jnp.right_shift(row, 6) == jnp.right_shift(col, 6), 1.0, 0.0).astype(BF16)
    wide_row = lax.broadcasted_iota(jnp.int32, (LANES, GDN_HEADS * LANES), 0)
    wide_head = jnp.right_shift(lax.broadcasted_iota(jnp.int32, (LANES, GDN_HEADS * LANES), 1), 7)
    pick_b = jnp.where(wide_row == wide_head + B_LANE, 1.0, 0.0).astype(BF16)
    a_row = -jnp.exp(alog_ref[layer:layer + 1, :])
    dt_row = dtb_ref[layer:layer + 1, :]
    nw = nw_ref[layer:layer + 1, :]
    heads = range(GDN_HEADS)
    hs = [slice(h * LANES, (h + 1) * LANES) for h in heads]
    ps = [slice(p * LANES, (p + 1) * LANES) for p in range(pairs)]

    @pl.when(first)
    def _():
        buf[...] = jnp.zeros(buf.shape, F32)


    def prepare(j):
        rows = slice(j * c, (j + 1) * c)

        def conv_silu(group):
            sl = slice(group * LANES, (group + 1) * LANES)
            x = qkv_ref[rows, sl]
            before = buf[:, sl] if j == 0 else qkv_ref[j * c - 8:j * c, sl]
            xx = jnp.concatenate([before, x], axis=0)
            y = x * cw_ref[GDN_CONV - 1:GDN_CONV, sl]
            for tap in range(GDN_CONV - 1):
                y = y + pltpu.roll(xx, GDN_CONV - 1 - tap, axis=0)[8:] * cw_ref[tap:tap + 1, sl]
            return _silu(y)

        m1 = m1_ref[rows, :]
        gc = _dot_exact(tril, a_row * _softplus(m1 + dt_row), 3)
        gc_rows = gc.T
        beta_wide = _dot_exact(_sigmoid(m1), pick_b, 2)
        yield
        out = []
        for p in range(pairs):
            xq, xk, xv = conv_silu(p), conv_silu(pairs + p), conv_silu(2 * pairs + p)
            yield
            qn = xq * (lax.rsqrt(_dot_exact(xq * xq, same_head, 2) + RMS_EPS) * (GDN_DK ** -0.5))
            kn = xk * lax.rsqrt(_dot_exact(xk * xk, same_head, 2) + RMS_EPS)
            yield
            for h in (2 * p, 2 * p + 1):
                mine = lo if h % 2 == 0 else jnp.logical_not(lo)
                gcb = jnp.broadcast_to(gc[:, A_LANE + h:A_LANE + h + 1], (c, LANES))
                gc_row = gc_rows[A_LANE + h:A_LANE + h + 1, :]
                out.append(dict(
                    q=jnp.where(mine, qn, 0.0), k=jnp.where(mine, kn, 0.0), v=jnp.where(mine, xv, 0.0),
                    gcb=gcb, beta=beta_wide[:, hs[h]],
                    decay=jnp.where(incl, jnp.exp(jnp.where(incl, gcb - gc_row, 0.0)), 0.0), e=jnp.exp(gcb)))
                yield
        return out

    def solve(pre):
        gram = [_dot_nt(jnp.concatenate([t["q"], t["k"]], axis=0).astype(BF16), t["k"].astype(BF16))
                for t in pre]
        yield
        l_strict = [jnp.where(strict, g[c:] * t["decay"] * t["beta"], 0.0) for g, t in zip(gram, pre)]
        t_inv = yield from _unit_lower_inverses(l_strict, row, col)
        out = []
        for g, t, ti in zip(gram, pre, t_inv):
            uw = _dot(ti.astype(BF16), jnp.concatenate([t["v"] * t["beta"], t["k"] * (t["beta"] * t["e"])],
                                                       axis=1).astype(BF16))
            g_end = t["gcb"][c - 1:c, :]
            out.append(dict(
                u=uw[:, :LANES], wq=jnp.concatenate([uw[:, LANES:], t["q"] * t["e"]], axis=0).astype(BF16),
                qk_kd=jnp.concatenate([g[:c] * t["decay"], (t["k"] * jnp.exp(g_end - t["gcb"])).T],
                                      axis=0).astype(BF16),
                keep=jnp.exp(g_end)))
        yield
        return out

    def advance(j, sol, state):
        ws = [_dot(s["wq"], st.astype(BF16)) for s, st in zip(sol, state)]
        yield
        fin = [_dot(s["qk_kd"], (s["u"] - w[:c]).astype(BF16)) for s, w in zip(sol, ws)]
        yield
        new_state = [st * s["keep"] + f[c:] for st, s, f in zip(state, sol, fin)]
        rows = slice(j * c, (j + 1) * c)
        for p in range(pairs):
            o_pair = (ws[2 * p][c:] + fin[2 * p][:c]) + (ws[2 * p + 1][c:] + fin[2 * p + 1][:c])
            ms = _dot_exact(o_pair * o_pair, same_head, 2) * (1.0 / GDN_DV)
            o_ref[rows, ps[p]] = (o_pair * lax.rsqrt(ms + RMS_EPS) * nw * _silu(z_ref[rows, ps[p]])).astype(o_ref.dtype)
            yield
        return new_state

    def prepare_group(js):
        out = []
        for j in js:
            out += yield from prepare(j)
        return out

    def group_matmuls(js, pre, state):
        sol = yield from solve(pre)
        for n, j in enumerate(js):
            state = yield from advance(j, sol[n * GDN_HEADS:(n + 1) * GDN_HEADS], state)
        return state

    def nothing():
        return None
        yield

    groups = [list(range(g0, min(g0 + GDN_GROUP, chunks))) for g0 in range(0, chunks, GDN_GROUP)]
    state = [state_ref[h] for h in heads]
    _, pre = _interleave(nothing(), prepare_group(groups[0]))
    for n, js in enumerate(groups):
        upcoming = prepare_group(groups[n + 1]) if n + 1 < len(groups) else nothing()
        state, pre = _interleave(group_matmuls(js, pre, state), upcoming)
    for h in heads:
        state_ref[h] = state[h]
    buf[...] = qkv_ref[chunks * c - 8:chunks * c, :]


def _gdn(layer, qkv, z, m1, conv_w, alog, dtb, nw, batch):
    t = qkv.shape[0]
    c = GDN_CHUNK * GDN_CHUNKS_PER_STEP
    nc = t // batch // c
    blk = lambda n: pl.BlockSpec((c, n), lambda b, i: (b * nc + i, 0))
    return pl.pallas_call(
        functools.partial(_gdn_kernel, layer),
        grid=(batch, nc),
        in_specs=[blk(N_QKV), blk(N_Z), blk(LANES), _of_layer(conv_w, layer),
                  _whole(alog), _whole(dtb), _whole(nw)],
        out_specs=blk(GDN_HEADS * GDN_DV),
        out_shape=jax.ShapeDtypeStruct((t, GDN_HEADS * GDN_DV), BF16),
        scratch_shapes=[pltpu.VMEM((8, N_QKV), F32), pltpu.VMEM((GDN_HEADS, LANES, LANES), F32)],
        compiler_params=_params(2),
        name="gdn",
    )(qkv, z, m1, conv_w, alog, dtb, nw)


def _sb_kernel(q_ref, k_ref, v_ref, o_ref):
    bq = q_ref.shape[0]
    g = LANES
    per = bq // g
    i = pl.program_id(2)
    lane = lax.broadcasted_iota(jnp.int32, (1, LANES), 1)
    lo = lane < HALF
    q = q_ref[...]
    row = lax.broadcasted_iota(jnp.int32, (bq, g), 0)
    col = lax.broadcasted_iota(jnp.int32, (bq, g), 1)
    jr = lax.broadcasted_iota(jnp.int32, (g, g), 0)
    jc = lax.broadcasted_iota(jnp.int32, (g, g), 1)
    later = jnp.where(jr > jc, 1.0, 0.0).astype(BF16)
    suffix_total = jnp.concatenate([later, jnp.ones((g, g), BF16)], axis=1)
    suffix_total = jnp.concatenate([suffix_total, suffix_total], axis=0)

    def group(start, run, acc, mask, r0):
        k_g = k_ref[pl.ds(start, g), :]
        v_g = v_ref[pl.ds(start, g), :]
        z = _dot_nt(q[r0:], _head_rows(k_g, lo))
        nl = jnp.maximum(z, 0.0) + jnp.log(1.0 + jnp.exp2(-jnp.abs(z))) * LOG2E
        if mask is not None:
            nl = jnp.where(mask, nl, 0.0)
        wts, new_run = [], []
        for hh in range(2):
            sl = slice(hh * g, (hh + 1) * g)
            hi, low = _split2(nl[:, sl])
            st = _dot(jnp.concatenate([hi, low], axis=1), suffix_total)
            wts.append(jnp.exp2(z[:, sl] - nl[:, sl] - (run[hh][r0:] + st[:, :g])))
            new_run.append(_replace_tail(run[hh], r0, run[hh][r0:] + st[:, g:]))
        wts = jnp.concatenate(wts, axis=1)
        if mask is not None:
            wts = jnp.where(mask, wts, 0.0)
        pv = _dot(wts.astype(BF16), _head_rows(v_g, lo))
        return tuple(new_run), _replace_tail(acc, r0, acc[r0:] + pv)

    run = (jnp.zeros((bq, g), F32), jnp.zeros((bq, g), F32))
    acc = jnp.zeros((bq, LANES), F32)
    base = i * bq
    for d in range(per):
        off = (per - 1 - d) * g
        mask = ((col + off) < row)[off:]
        run, acc = group(pl.multiple_of(base + off, g), run, acc, jnp.concatenate([mask, mask], axis=1), off)

    def block(n, carry):
        run, acc = carry
        start = (i - 1 - n) * bq
        for d in range(per):
            run, acc = group(pl.multiple_of(start + (per - 1 - d) * g, g), run, acc, None, 0)
        return run, acc

    run, acc = lax.fori_loop(0, i, block, (run, acc))
    o_ref[...] = acc.astype(o_ref.dtype)


def _sb_attn(sb, batch):
    t = sb.shape[0]
    s = t // batch
    bq = SB_QUERY_BLOCK
    nq = s // bq
    pairs = SB_HEADS // 2
    return pl.pallas_call(
        _sb_kernel,
        grid=(batch, pairs, nq),
        in_specs=[pl.BlockSpec((bq, LANES), lambda b, p, i: (b * nq + i, p)),
                  pl.BlockSpec((s, LANES), lambda b, p, i: (b, pairs + p)),
                  pl.BlockSpec((s, LANES), lambda b, p, i: (b, 2 * pairs + p))],
        out_specs=pl.BlockSpec((bq, LANES), lambda b, p, i: (b * nq + i, p)),
        out_shape=jax.ShapeDtypeStruct((t, SB_HEADS * SB_DIM), BF16),
        compiler_params=_params(3),
        name="sb_attn",
    )(sb, sb, sb)


def _mla_kernel(q_ref, k_ref, v_ref, o_ref):
    bq = q_ref.shape[0]
    bk = MLA_KEY_BLOCK
    per = bq // bk
    i = pl.program_id(2)
    lane = lax.broadcasted_iota(jnp.int32, (1, LANES), 1)
    lo = lane < HALF
    q = q_ref[...]
    row = lax.broadcasted_iota(jnp.int32, (bq, bk), 0)
    col = lax.broadcasted_iota(jnp.int32, (bq, bk), 1)

    def visit(start, carry, mask, r0):
        stats, acc = carry
        k_b = k_ref[pl.ds(start, bk), :]
        v_b = v_ref[pl.ds(start, bk), :]
        new, probs, alphas = [], [], []
        for hh in range(2):
            m, l = stats[hh]
            sl = slice(hh * LANES, (hh + 1) * LANES)
            s = _dot_nt(q[r0:, sl], k_b[:, sl])
            if mask is not None:
                s = jnp.where(mask, s, -jnp.inf)
            m_new = jnp.maximum(m[r0:], jnp.max(s, axis=-1, keepdims=True))
            alpha = jnp.exp2(m[r0:] - m_new)
            p = jnp.exp2(s - m_new)
            new.append((_replace_tail(m, r0, m_new),
                        _replace_tail(l, r0, alpha * l[r0:] + jnp.sum(p, axis=-1, keepdims=True))))
            probs.append(p.astype(BF16))
            alphas.append(alpha)
        pv = _dot(jnp.concatenate(probs, axis=1), _head_rows(v_b, lo))
        return tuple(new), _replace_tail(acc, r0, jnp.where(lo, alphas[0], alphas[1]) * acc[r0:] + pv)

    carry = (tuple((jnp.full((bq, 1), -jnp.inf, F32), jnp.zeros((bq, 1), F32)) for _ in range(2)),
             jnp.zeros((bq, LANES), F32))
    base = i * bq
    for d in range(per):
        mask = ((col + d * bk) <= row)[d * bk:]
        carry = visit(pl.multiple_of(base + d * bk, bk), carry, mask, d * bk)

    def block(n, cr):
        for d in range(per):
            cr = visit(pl.multiple_of(n * bq + d * bk, bk), cr, None, 0)
        return cr

    stats, acc = lax.fori_loop(0, i, block, carry)
    o_ref[...] = (acc / jnp.where(lo, stats[0][1], stats[1][1])).astype(o_ref.dtype)


def _mla_attn(q, k, v, batch):
    t = q.shape[0]
    s = t // batch
    bq = MLA_BLOCK
    nq = s // bq
    pairs = MLA_HEADS // 2
    return pl.pallas_call(
        _mla_kernel,
        grid=(batch, pairs, nq),
        in_specs=[pl.BlockSpec((bq, 2 * LANES), lambda b, p, i: (b * nq + i, p)),
                  pl.BlockSpec((s, 2 * LANES), lambda b, p, i: (b, p)),
                  pl.BlockSpec((s, LANES), lambda b, p, i: (b, p))],
        out_specs=pl.BlockSpec((bq, LANES), lambda b, p, i: (b * nq + i, p)),
        out_shape=jax.ShapeDtypeStruct((t, MLA_HEADS * MLA_V), BF16),
        compiler_params=_params(3),
        name="mla_attn",
    )(q, k, v)


def _post_kernel(h_ref, og_ref, os_ref, om_ref, w_o_ref, g_ref, b_ref, w_in_ref, w_out_ref,
                 p_ref, w_g_ref, w_p_ref, out_ref):
    n_g = GDN_HEADS * GDN_DV
    n_s = SB_HEADS * SB_DIM
    x = h_ref[...]
    mix = (_dot(og_ref[...], w_o_ref[:n_g, :]) + _dot(os_ref[...], w_o_ref[n_g:n_g + n_s, :])
           + _dot(om_ref[...], w_o_ref[n_g + n_s:, :]))
    h1 = _layer_norm(DEEPNORM_ALPHA * x + mix, g_ref[1:2, :], b_ref[1:2, :])
    y = _swiglu(h1.astype(BF16), w_in_ref, w_out_ref)
    h2 = _layer_norm(DEEPNORM_ALPHA * h1 + 0.5 * y, g_ref[2:3, :], b_ref[2:3, :])
    gate = _sigmoid(_dot(h2.astype(BF16), w_g_ref[...]))
    out_ref[...] = h2 + gate * _dot(p_ref[...].astype(BF16), w_p_ref[...])


def _post(layer, h, o_gdn, o_sb, o_mla, w_o, ln_g, ln_b, w_in, w_out, p, w_g, w_p):
    t = h.shape[0]
    tm = TOKEN_TILE
    row = lambda n: pl.BlockSpec((tm, n), lambda i: (i, 0))
    return pl.pallas_call(
        _post_kernel,
        grid=(t // tm,),
        in_specs=[row(D_MODEL), row(o_gdn.shape[1]), row(o_sb.shape[1]), row(o_mla.shape[1]),
                  _of_layer(w_o, layer), _of_layer(ln_g, layer), _of_layer(ln_b, layer),
                  _of_layer(w_in, layer), _of_layer(w_out, layer),
                  pl.BlockSpec((None, tm, PLE_DIM), lambda i: (layer, i, 0)),
                  _of_layer(w_g, layer), _of_layer(w_p, layer)],
        out_specs=row(D_MODEL),
        out_shape=jax.ShapeDtypeStruct((t, D_MODEL), F32),
        compiler_params=_params(1),
        name="post",
    )(h, o_gdn, o_sb, o_mla, w_o, ln_g, ln_b, w_in, w_out, p, w_g, w_p)


def _zeros_like_cols(w, n):
    return jnp.zeros(w.shape[:-1] + (n,), w.dtype)


def _regroup_mix(w):
    edges = [N_QKV + N_Z, GDN_HEADS, GDN_HEADS, N_SB + MLA_Q_RANK + MLA_KV_RANK, MLA_ROPE]
    parts, o = [], 0
    for n in edges:
        parts.append(w[..., o:o + n])
        o += n
    gdn, ga, gb, att, kr = parts
    half = MLA_ROPE // 2
    pad = LANES - MLA_NOPE - MLA_ROPE
    m1 = [ga, gb, _zeros_like_cols(w, MLA_NOPE - 2 * GDN_HEADS), kr, _zeros_like_cols(w, pad)]
    m2 = [_zeros_like_cols(w, MLA_NOPE), kr[..., half:], kr[..., :half], _zeros_like_cols(w, pad)]
    return gdn.astype(BF16), att.astype(BF16), jnp.concatenate(m1 + m2, axis=-1).astype(BF16)


def _regroup_uq(w):
    d = MLA_NOPE + MLA_ROPE
    half = MLA_ROPE // 2
    pad = _zeros_like_cols(w, LANES - d)
    main, swap = [], []
    for h in range(MLA_HEADS):
        main += [w[..., h * d:(h + 1) * d], pad]
        swap += [_zeros_like_cols(w, MLA_NOPE), w[..., h * d + MLA_NOPE + half:(h + 1) * d],
                 w[..., h * d + MLA_NOPE:h * d + MLA_NOPE + half], pad]
    return jnp.concatenate(main + swap, axis=-1)


def _regroup_ukv(w):
    d = MLA_NOPE + MLA_V
    keys, vals = [], []
    for h in range(MLA_HEADS):
        keys += [w[..., h * d:h * d + MLA_NOPE], _zeros_like_cols(w, LANES - MLA_NOPE)]
        vals.append(w[..., h * d + MLA_NOPE:(h + 1) * d])
    return jnp.concatenate(keys + vals, axis=-1)


def _lane_rows(vals, start):
    return jnp.pad(vals.astype(F32), ((0, 0), (start, LANES - start - vals.shape[1])))


def _rope_rows():
    inv = 1.0 / (ROPE_BASE ** (jnp.arange(0, MLA_ROPE, 2, dtype=F32) / MLA_ROPE))
    half = MLA_ROPE // 2
    pad = LANES - MLA_NOPE - MLA_ROPE
    z = lambda n: jnp.zeros((n,), F32)
    one = lambda n: jnp.ones((n,), F32)
    freq = jnp.concatenate([z(MLA_NOPE), inv, inv, z(pad)])
    cos_mask = jnp.concatenate([one(MLA_NOPE + MLA_ROPE), z(pad)])
    sin_sign = jnp.concatenate([z(MLA_NOPE), -one(half), one(half), z(pad)])
    rows = jnp.stack([freq, cos_mask, sin_sign])
    return jnp.pad(rows, ((0, ROPE_ROWS - rows.shape[0]), (0, 0)))


def kernel(x, p, positions, ffa_w_in, ffa_w_out, mix_w_in, gdn_conv_w, gdn_a_log, gdn_dt_bias, gdn_norm_w, mla_q_norm_w, mla_kv_norm_w, mla_w_uq, mla_w_ukv, mix_w_o, ffb_w_in, ffb_w_out, ln_g, ln_b, ple_w_gate, ple_w_proj):
    batch, seq, d = x.shape
    t = batch * seq
    depth = p.shape[0]
    hd = GDN_HEADS * GDN_DK

    bf = lambda w: w.astype(BF16)
    w_mix, w_uq, w_ukv = _regroup_mix(mix_w_in), bf(_regroup_uq(mla_w_uq)), bf(_regroup_ukv(mla_w_ukv))
    ffa_in, ffa_out, ffb_in, ffb_out = bf(ffa_w_in), bf(ffa_w_out), bf(ffb_w_in), bf(ffb_w_out)
    w_o, w_g, w_p = bf(mix_w_o), bf(ple_w_gate), bf(ple_w_proj)
    alog, dtb = _lane_rows(gdn_a_log, A_LANE), _lane_rows(gdn_dt_bias, A_LANE)
    nw = jnp.concatenate([gdn_norm_w, gdn_norm_w], axis=-1).astype(F32)
    rope = _rope_rows()
    pos = positions.reshape(t, 1).astype(F32)
    p_rows = p.reshape(depth, t, p.shape[-1])

    h = x.reshape(t, d)
    for i in range(depth):
        h, qkv, z, sb, m1, q_m, k_m, v_m = _ffn_inproj(i, h, pos, rope, ffa_in, ffa_out, ln_g, ln_b, w_mix,
                                                       mla_q_norm_w, mla_kv_norm_w, w_uq, w_ukv)
        o_gdn = _gdn(i, qkv, z, m1, gdn_conv_w, alog, dtb, nw, batch)
        o_sb = _sb_attn(sb, batch)
        o_mla = _mla_attn(q_m, k_m, v_m, batch)
        h = _post(i, h, o_gdn, o_sb, o_mla, w_o, ln_g, ln_b, ffb_in, ffb_out, p_rows, w_g, w_p)
    return h.reshape(batch, seq, d)
```

```python
import functools

import jax
import jax.numpy as jnp
from jax import lax
from jax.experimental import pallas as pl
from jax.experimental.pallas import tpu as pltpu

F32 = jnp.float32
BF16 = jnp.bfloat16

DEPTH = 2
D_MODEL = 1024
PLE_DIM = 256
D_FF = 2816
LN_EPS = 1e-5
RMS_EPS = 1e-6
DEEPNORM_ALPHA = (2 * DEPTH) ** 0.25

GDN_HEADS = 8
GDN_DK = 64
GDN_DV = 64
GDN_CONV = 4
SB_HEADS = 4
SB_DIM = 64
MLA_HEADS = 4
MLA_NOPE = 64
MLA_ROPE = 32
MLA_V = 64
MLA_Q_RANK = 256
MLA_KV_RANK = 128
ROPE_BASE = 10000.0

LOG2E = 1.4426950408889634
LANES = 128
HALF = LANES // 2
VMEM_LIMIT = 56 * 1024 * 1024

GDN_CHUNK = 128
GDN_CHUNKS_PER_STEP = 4
GDN_GROUP = 2
TOKEN_TILE = 512
SB_QUERY_BLOCK = 512
SB_STEP_GROUPS = 2
SB_DEAD_LOG2 = 150.0
MLA_BLOCK = 1024
MLA_KEY_BLOCK = 1024

N_HD = GDN_HEADS * GDN_DK
N_QKV = 3 * N_HD
N_Z = GDN_HEADS * GDN_DV
N_SB = 3 * SB_HEADS * SB_DIM
N_ML = MLA_Q_RANK + MLA_KV_RANK + 2 * LANES
A_LANE = 0
B_LANE = GDN_HEADS
N_MIX = N_QKV + N_Z + N_SB + N_ML
N_MLA_QK = MLA_HEADS * LANES
N_MLA_V = MLA_HEADS * MLA_V
ROPE_ROWS = 8


def _silu(x):
    return x / (1.0 + jnp.exp(-x))


def _sigmoid(x):
    return 1.0 / (1.0 + jnp.exp(-x))


def _softplus(x):
    return jnp.maximum(x, 0.0) + jnp.log1p(jnp.exp(-jnp.abs(x)))


def _layer_norm(r, g, b):
    mu = jnp.mean(r, axis=-1, keepdims=True)
    d = r - mu
    var = jnp.mean(d * d, axis=-1, keepdims=True)
    return d * lax.rsqrt(var + LN_EPS) * g + b


def _rms_norm(x, w):
    return x * lax.rsqrt(jnp.mean(x * x, axis=-1, keepdims=True) + RMS_EPS) * w


def _dot(a, b):
    return jnp.dot(a, b, preferred_element_type=F32)


def _dot_nt(a, b):
    return lax.dot_general(a, b, (((1,), (1,)), ((), ())), preferred_element_type=F32)


def _head_rows(x, lo):
    zero = jnp.zeros_like(x)
    return jnp.concatenate([jnp.where(lo, x, zero), jnp.where(lo, zero, x)], axis=0)


def _replace_tail(full, r0, tail):
    return tail if r0 == 0 else jnp.concatenate([full[:r0], tail], axis=0)


def _split2(x):
    hi = x.astype(BF16)
    lo = (x - hi.astype(F32)).astype(BF16)
    return hi, lo


def _split3(x):
    hi = x.astype(BF16)
    r = x - hi.astype(F32)
    mid = r.astype(BF16)
    lo = (r - mid.astype(F32)).astype(BF16)
    return hi, mid, lo


def _swiglu(xb, w_in_ref, w_out_ref):
    gu = _dot(xb, w_in_ref[...])
    act = _silu(gu[:, :D_FF]) * gu[:, D_FF:]
    return _dot(act.astype(BF16), w_out_ref[...])


def _whole(a):
    return pl.BlockSpec(a.shape, lambda *_: (0,) * a.ndim, pipeline_mode=pl.Buffered(1))


def _of_layer(a, layer, block=None, at=None):
    tail = tuple(a.shape[1:]) if block is None else tuple(block)
    idx = (0,) * len(tail) if at is None else tuple(at)
    return pl.BlockSpec((None,) + tail, lambda *_: (layer,) + idx, pipeline_mode=pl.Buffered(1))


def _params(n_axes):
    return pltpu.CompilerParams(dimension_semantics=("arbitrary",) * n_axes, vmem_limit_bytes=VMEM_LIMIT)


def _ffn_inproj_kernel(layer, h_ref, pos_ref, rope_ref, w_in_ref, w_out_ref, g_ref, b_ref,
                       w_gdn_ref, w_att_ref, w_m_ref, qnw_ref, kvnw_ref, w_uq_ref, w_ukv_ref,
                       h_out, qkv_out, z_out, sb_out, m1_out, mq_out, mk_out, mv_out):
    x = h_ref[...]
    y = _swiglu(x.astype(BF16), w_in_ref, w_out_ref)
    hn = _layer_norm(DEEPNORM_ALPHA * x + 0.5 * y, g_ref[0:1, :], b_ref[0:1, :])
    h_out[...] = hn
    hb = hn.astype(BF16)
    gdn = _dot(hb, w_gdn_ref[...])
    qkv_out[...] = gdn[:, :N_QKV]
    z_out[...] = gdn[:, N_QKV:]
    att = _dot(hb, w_att_ref[...])
    n_q = SB_HEADS * SB_DIM
    sb_out[:, :n_q] = (att[:, :n_q] * (SB_DIM ** -0.5 * LOG2E)).astype(BF16)
    sb_out[:, n_q:] = att[:, n_q:N_SB].astype(BF16)
    mq = att[:, N_SB:N_SB + MLA_Q_RANK]
    ckv = att[:, N_SB + MLA_Q_RANK:]
    m12 = _dot(hb, w_m_ref[...])
    m1 = m12[:, :LANES]
    m2 = m12[:, LANES:]
    m1_out[...] = m1

    ang = pos_ref[...] * rope_ref[0:1, :]
    cos = jnp.cos(ang) * rope_ref[1:2, :]
    sin = jnp.sin(ang) * rope_ref[2:3, :]
    lane = lax.broadcasted_iota(jnp.int32, (1, LANES), 1)
    rot = (lane >= MLA_NOPE) & (lane < MLA_NOPE + MLA_ROPE)

    qf = _dot(_rms_norm(mq, qnw_ref[layer:layer + 1, :]).astype(BF16), w_uq_ref[...])
    kv = _dot(_rms_norm(ckv, kvnw_ref[layer:layer + 1, :]).astype(BF16), w_ukv_ref[...])
    scale = (MLA_NOPE + MLA_ROPE) ** -0.5 * LOG2E
    k_rot = jnp.where(rot, m1 * cos, 0.0) + m2 * sin
    for h in range(MLA_HEADS):
        sl = slice(h * LANES, (h + 1) * LANES)
        q_h = qf[:, sl] * cos + qf[:, N_MLA_QK + h * LANES:N_MLA_QK + (h + 1) * LANES] * sin
        mq_out[:, sl] = (q_h * scale).astype(BF16)
        mk_out[:, sl] = (kv[:, sl] + k_rot).astype(BF16)
    mv_out[...] = kv[:, N_MLA_QK:].astype(BF16)


def _ffn_inproj(layer, h, pos, rope, w_in, w_out, ln_g, ln_b, w_mix, qnw, kvnw, w_uq, w_ukv):
    w_gdn, w_att, w_m = w_mix
    t = h.shape[0]
    tm = TOKEN_TILE
    row = lambda n: pl.BlockSpec((tm, n), lambda i: (i, 0))
    widths = (D_MODEL, N_QKV, N_Z, N_SB, LANES, N_MLA_QK, N_MLA_QK, N_MLA_V)
    dtypes = (F32, F32, F32, BF16, F32, BF16, BF16, BF16)
    return pl.pallas_call(
        functools.partial(_ffn_inproj_kernel, layer),
        grid=(t // tm,),
        in_specs=[row(D_MODEL), row(1), _whole(rope), _of_layer(w_in, layer), _of_layer(w_out, layer),
                  _of_layer(ln_g, layer), _of_layer(ln_b, layer),
                  _of_layer(w_gdn, layer), _of_layer(w_att, layer), _of_layer(w_m, layer),
                  _whole(qnw), _whole(kvnw), _of_layer(w_uq, layer), _of_layer(w_ukv, layer)],
        out_specs=[row(n) for n in widths],
        out_shape=[jax.ShapeDtypeStruct((t, n), dt) for n, dt in zip(widths, dtypes)],
        compiler_params=_params(1),
        name="ffn_inproj",
    )(h, pos, rope, w_in, w_out, ln_g, ln_b, w_gdn, w_att, w_m, qnw, kvnw, w_uq, w_ukv)


def _unit_lower_inverses(l_strict, row, col):
    c = l_strict[0].shape[0]
    blk = lambda v, log2: jnp.right_shift(v, log2)
    pair = (blk(row, 1) == blk(col, 1)) & (row > col)
    eye = jnp.where(row == col, 1.0, 0.0)
    xs = [eye - jnp.where(pair, l, 0.0) for l in l_strict]
    log2 = 1
    while (2 << log2) <= c:
        off = (blk(row, log2 + 1) == blk(col, log2 + 1)) & (blk(row, log2) > blk(col, log2))
        xb = [x.astype(BF16) for x in xs]
        lx = [_dot(jnp.where(off, l, 0.0).astype(BF16), b).astype(BF16) for l, b in zip(l_strict, xb)]
        yield
        xs = [x - _dot(b, t) for x, b, t in zip(xs, xb, lx)]
        yield
        log2 += 1
    return xs


def _interleave(main, side):
    values, live = [None, None], [main, side]
    while any(g is not None for g in live):
        for n, g in enumerate(live):
            if g is not None:
                try:
                    next(g)
                except StopIteration as stop:
                    values[n], live[n] = stop.value, None
    return values


def _dot_exact(a, b, terms):
    split = {2: _split2, 3: _split3}[terms]
    if a.dtype == F32:
        parts = [_dot(piece, b) for piece in split(a)]
    else:
        parts = [_dot(a, piece) for piece in split(b)]
    return functools.reduce(lambda x, y: x + y, parts)


def _gdn_kernel(layer, qkv_ref, z_ref, m1_ref, cw_ref, alog_ref, dtb_ref, nw_ref, o_ref, buf, state_ref):
    c = GDN_CHUNK
    chunks = qkv_ref.shape[0] // c
    pairs = GDN_HEADS // 2
    first = pl.program_id(1) == 0

    @pl.when(first)
    def _():
        state_ref[...] = jnp.zeros(state_ref.shape, F32)

    lane = lax.broadcasted_iota(jnp.int32, (1, LANES), 1)
    lo = lane < HALF
    row = lax.broadcasted_iota(jnp.int32, (c, c), 0)
    col = lax.broadcasted_iota(jnp.int32, (c, c), 1)
    incl = row >= col
    strict = row > col
    tril = jnp.where(incl, 1.0, 0.0).astype(BF16)
    same_head = jnp.where(jnp.right_shift(row, 6) == jnp.right_shift(col, 6), 1.0, 0.0).astype(BF16)
    wide_row = lax.broadcasted_iota(jnp.int32, (LANES, GDN_HEADS * LANES), 0)
    wide_head = jnp.right_shift(lax.broadcasted_iota(jnp.int32, (LANES, GDN_HEADS * LANES), 1), 7)
    pick_b = jnp.where(wide_row == wide_head + B_LANE, 1.0, 0.0).astype(BF16)
    a_row = -jnp.exp(alog_ref[layer:layer + 1, :])
    dt_row = dtb_ref[layer:layer + 1, :]
    nw = nw_ref[layer:layer + 1, :]
    heads = range(GDN_HEADS)
    hs = [slice(h * LANES, (h + 1) * LANES) for h in heads]
    ps = [slice(p * LANES, (p + 1) * LANES) for p in range(pairs)]

    @pl.when(first)
    def _():
        buf[...] = jnp.zeros(buf.shape, F32)


    def prepare(j):
        rows = slice(j * c, (j + 1) * c)

        def conv_silu(group):
            sl = slice(group * LANES, (group + 1) * LANES)
            x = qkv_ref[rows, sl]
            before = buf[:, sl] if j == 0 else qkv_ref[j * c - 8:j * c, sl]
            xx = jnp.concatenate([before, x], axis=0)
            y = x * cw_ref[GDN_CONV - 1:GDN_CONV, sl]
            for tap in range(GDN_CONV - 1):
                y = y + pltpu.roll(xx, GDN_CONV - 1 - tap, axis=0)[8:] * cw_ref[tap:tap + 1, sl]
            return _silu(y)

        m1 = m1_ref[rows, :]
        gc = _dot_exact(tril, a_row * _softplus(m1 + dt_row), 3)
        gc_rows = gc.T
        beta_wide = _dot_exact(_sigmoid(m1), pick_b, 2)
        yield
        out = []
        for p in range(pairs):
            xq, xk, xv = conv_silu(p), conv_silu(pairs + p), conv_silu(2 * pairs + p)
            yield
            qn = xq * (lax.rsqrt(_dot_exact(xq * xq, same_head, 2) + RMS_EPS) * (GDN_DK ** -0.5))
            kn = xk * lax.rsqrt(_dot_exact(xk * xk, same_head, 2) + RMS_EPS)
            yield
            for h in (2 * p, 2 * p + 1):
                mine = lo if h % 2 == 0 else jnp.logical_not(lo)
                gcb = jnp.broadcast_to(gc[:, A_LANE + h:A_LANE + h + 1], (c, LANES))
                gc_row = gc_rows[A_LANE + h:A_LANE + h + 1, :]
                out.append(dict(
                    q=jnp.where(mine, qn, 0.0), k=jnp.where(mine, kn, 0.0), v=jnp.where(mine, xv, 0.0),
                    gcb=gcb, beta=beta_wide[:, hs[h]],
                    decay=jnp.where(incl, jnp.exp(jnp.where(incl, gcb - gc_row, 0.0)), 0.0), e=jnp.exp(gcb)))
                yield
        return out

    def solve(pre):
        gram = [_dot_nt(jnp.concatenate([t["q"], t["k"]], axis=0).astype(BF16), t["k"].astype(BF16))
                for t in pre]
        yield
        l_strict = [jnp.where(strict, g[c:] * t["decay"] * t["beta"], 0.0) for g, t in zip(gram, pre)]
        t_inv = yield from _unit_lower_inverses(l_strict, row, col)
        out = []
        for g, t, ti in zip(gram, pre, t_inv):
            uw = _dot(ti.astype(BF16), jnp.concatenate([t["v"] * t["beta"], t["k"] * (t["beta"] * t["e"])],
                                                       axis=1).astype(BF16))
            g_end = t["gcb"][c - 1:c, :]
            out.append(dict(
                u=uw[:, :LANES], wq=jnp.concatenate([uw[:, LANES:], t["q"] * t["e"]], axis=0).astype(BF16),
                qk_kd=jnp.concatenate([g[:c] * t["decay"], (t["k"] * jnp.exp(g_end - t["gcb"])).T],
                                      axis=0).astype(BF16),
                keep=jnp.exp(g_end)))
        yield
        return out

    def advance(j, sol, state):
        ws = [_dot(s["wq"], st.astype(BF16)) for s, st in zip(sol, state)]
        yield
        fin = [_dot(s["qk_kd"], (s["u"] - w[:c]).astype(BF16)) for s, w in zip(sol, ws)]
        yield
        new_state = [st * s["keep"] + f[c:] for st, s, f in zip(state, sol, fin)]
        rows = slice(j * c, (j + 1) * c)
        for p in range(pairs):
            o_pair = (ws[2 * p][c:] + fin[2 * p][:c]) + (ws[2 * p + 1][c:] + fin[2 * p + 1][:c])
            ms = _dot_exact(o_pair * o_pair, same_head, 2) * (1.0 / GDN_DV)
            o_ref[rows, ps[p]] = (o_pair * lax.rsqrt(ms + RMS_EPS) * nw * _silu(z_ref[rows, ps[p]])).astype(o_ref.dtype)
            yield
        return new_state

    def prepare_group(js):
        out = []
        for j in js:
            out += yield from prepare(j)
        return out

    def group_matmuls(js, pre, state):
        sol = yield from solve(pre)
        for n, j in enumerate(js):
            state = yield from advance(j, sol[n * GDN_HEADS:(n + 1) * GDN_HEADS], state)
        return state

    def nothing():
        return None
        yield

    groups = [list(range(g0, min(g0 + GDN_GROUP, chunks))) for g0 in range(0, chunks, GDN_GROUP)]
    state = [state_ref[h] for h in heads]
    _, pre = _interleave(nothing(), prepare_group(groups[0]))
    for n, js in enumerate(groups):
        upcoming = prepare_group(groups[n + 1]) if n + 1 < len(groups) else nothing()
        state, pre = _interleave(group_matmuls(js, pre, state), upcoming)
    for h in heads:
        state_ref[h] = state[h]
    buf[...] = qkv_ref[chunks * c - 8:chunks * c, :]


def _gdn(layer, qkv, z, m1, conv_w, alog, dtb, nw, batch):
    t = qkv.shape[0]
    c = GDN_CHUNK * GDN_CHUNKS_PER_STEP
    nc = t // batch // c
    blk = lambda n: pl.BlockSpec((c, n), lambda b, i: (b * nc + i, 0))
    return pl.pallas_call(
        functools.partial(_gdn_kernel, layer),
        grid=(batch, nc),
        in_specs=[blk(N_QKV), blk(N_Z), blk(LANES), _of_layer(conv_w, layer),
                  _whole(alog), _whole(dtb), _whole(nw)],
        out_specs=blk(GDN_HEADS * GDN_DV),
        out_shape=jax.ShapeDtypeStruct((t, GDN_HEADS * GDN_DV), BF16),
        scratch_shapes=[pltpu.VMEM((8, N_QKV), F32), pltpu.VMEM((GDN_HEADS, LANES, LANES), F32)],
        compiler_params=_params(2),
        name="gdn",
    )(qkv, z, m1, conv_w, alog, dtb, nw)


def _sb_kernel(q_ref, k_ref, v_ref, o_ref):
    bq = q_ref.shape[0]
    g = LANES
    per = bq // g
    i = pl.program_id(2)
    lane = lax.broadcasted_iota(jnp.int32, (1, LANES), 1)
    lo = lane < HALF
    q = q_ref[...]
    row = lax.broadcasted_iota(jnp.int32, (bq, g), 0)
    col = lax.broadcasted_iota(jnp.int32, (bq, g), 1)
    jr = lax.broadcasted_iota(jnp.int32, (g, g), 0)
    jc = lax.broadcasted_iota(jnp.int32, (g, g), 1)
    later = jnp.where(jr > jc, 1.0, 0.0).astype(BF16)
    suffix_total = jnp.concatenate([later, jnp.ones((g, g), BF16)], axis=1)
    suffix_total = jnp.concatenate([suffix_total, suffix_total], axis=0)

    def group(start, run, acc, mask, r0):
        k_g = k_ref[pl.ds(start, g), :]
        v_g = v_ref[pl.ds(start, g), :]
        z = _dot_nt(q[r0:], _head_rows(k_g, lo))
        nl = jnp.maximum(z, 0.0) + jnp.log(1.0 + jnp.exp2(-jnp.abs(z))) * LOG2E
        if mask is not None:
            nl = jnp.where(mask, nl, 0.0)
        wts, new_run = [], []
        for hh in range(2):
            sl = slice(hh * g, (hh + 1) * g)
            hi, low = _split2(nl[:, sl])
            st = _dot(jnp.concatenate([hi, low], axis=1), suffix_total)
            wts.append(jnp.exp2(z[:, sl] - nl[:, sl] - (run[hh][r0:] + st[:, :g])))
            new_run.append(_replace_tail(run[hh], r0, run[hh][r0:] + st[:, g:]))
        wts = jnp.concatenate(wts, axis=1)
        if mask is not None:
            wts = jnp.where(mask, wts, 0.0)
        pv = _dot(wts.astype(BF16), _head_rows(v_g, lo))
        return tuple(new_run), _replace_tail(acc, r0, acc[r0:] + pv)

    run = (jnp.zeros((bq, g), F32), jnp.zeros((bq, g), F32))
    acc = jnp.zeros((bq, LANES), F32)
    base = i * bq
    for d in range(per):
        off = (per - 1 - d) * g
        mask = ((col + off) < row)[off:]
        run, acc = group(pl.multiple_of(base + off, g), run, acc, jnp.concatenate([mask, mask], axis=1), off)

    step = SB_STEP_GROUPS * g
    steps = i * (bq // step)

    def lowest(run):
        return jnp.min(jnp.minimum(run[0], run[1]))

    def more(carry):
        n, low, _, _ = carry
        return jnp.logical_and(n < steps, low < SB_DEAD_LOG2)

    def walk(carry):
        n, _, run, acc = carry
        start = base - (n + 1) * step
        for d in range(SB_STEP_GROUPS):
            run, acc = group(pl.multiple_of(start + (SB_STEP_GROUPS - 1 - d) * g, g), run, acc, None, 0)
        return n + 1, lowest(run), run, acc

    _, _, run, acc = lax.while_loop(more, walk, (jnp.int32(0), lowest(run), run, acc))
    o_ref[...] = acc.astype(o_ref.dtype)


def _sb_attn(sb, batch):
    t = sb.shape[0]
    s = t // batch
    bq = SB_QUERY_BLOCK
    nq = s // bq
    pairs = SB_HEADS // 2
    return pl.pallas_call(
        _sb_kernel,
        grid=(batch, pairs, nq),
        in_specs=[pl.BlockSpec((bq, LANES), lambda b, p, i: (b * nq + i, p)),
                  pl.BlockSpec((s, LANES), lambda b, p, i: (b, pairs + p)),
                  pl.BlockSpec((s, LANES), lambda b, p, i: (b, 2 * pairs + p))],
        out_specs=pl.BlockSpec((bq, LANES), lambda b, p, i: (b * nq + i, p)),
        out_shape=jax.ShapeDtypeStruct((t, SB_HEADS * SB_DIM), BF16),
        compiler_params=_params(3),
        name="sb_attn",
    )(sb, sb, sb)


def _mla_kernel(q_ref, k_ref, v_ref, o_ref):
    bq = q_ref.shape[0]
    bk = MLA_KEY_BLOCK
    per = bq // bk
    i = pl.program_id(2)
    lane = lax.broadcasted_iota(jnp.int32, (1, LANES), 1)
    lo = lane < HALF
    q = q_ref[...]
    row = lax.broadcasted_iota(jnp.int32, (bq, bk), 0)
    col = lax.broadcasted_iota(jnp.int32, (bq, bk), 1)

    def visit(start, carry, mask, r0):
        stats, acc = carry
        k_b = k_ref[pl.ds(start, bk), :]
        v_b = v_ref[pl.ds(start, bk), :]
        new, probs, alphas = [], [], []
        for hh in range(2):
            m, l = stats[hh]
            sl = slice(hh * LANES, (hh + 1) * LANES)
            s = _dot_nt(q[r0:, sl], k_b[:, sl])
            if mask is not None:
                s = jnp.where(mask, s, -jnp.inf)
            m_new = jnp.maximum(m[r0:], jnp.max(s, axis=-1, keepdims=True))
            alpha = jnp.exp2(m[r0:] - m_new)
            p = jnp.exp2(s - m_new)
            new.append((_replace_tail(m, r0, m_new),
                        _replace_tail(l, r0, alpha * l[r0:] + jnp.sum(p, axis=-1, keepdims=True))))
            probs.append(p.astype(BF16))
            alphas.append(alpha)
        pv = _dot(jnp.concatenate(probs, axis=1), _head_rows(v_b, lo))
        return tuple(new), _replace_tail(acc, r0, jnp.where(lo, alphas[0], alphas[1]) * acc[r0:] + pv)

    carry = (tuple((jnp.full((bq, 1), -jnp.inf, F32), jnp.zeros((bq, 1), F32)) for _ in range(2)),
             jnp.zeros((bq, LANES), F32))
    base = i * bq
    for d in range(per):
        mask = ((col + d * bk) <= row)[d * bk:]
        carry = visit(pl.multiple_of(base + d * bk, bk), carry, mask, d * bk)

    def block(n, cr):
        for d in range(per):
            cr = visit(pl.multiple_of(n * bq + d * bk, bk), cr, None, 0)
        return cr

    stats, acc = lax.fori_loop(0, i, block, carry)
    o_ref[...] = (acc / jnp.where(lo, stats[0][1], stats[1][1])).astype(o_ref.dtype)


def _mla_attn(q, k, v, batch):
    t = q.shape[0]
    s = t // batch
    bq = MLA_BLOCK
    nq = s // bq
    pairs = MLA_HEADS // 2
    return pl.pallas_call(
        _mla_kernel,
        grid=(batch, pairs, nq),
        in_specs=[pl.BlockSpec((bq, 2 * LANES), lambda b, p, i: (b * nq + i, p)),
                  pl.BlockSpec((s, 2 * LANES), lambda b, p, i: (b, p)),
                  pl.BlockSpec((s, LANES), lambda b, p, i: (b, p))],
        out_specs=pl.BlockSpec((bq, LANES), lambda b, p, i: (b * nq + i, p)),
        out_shape=jax.ShapeDtypeStruct((t, MLA_HEADS * MLA_V), BF16),
        compiler_params=_params(3),
        name="mla_attn",
    )(q, k, v)


def _post_kernel(h_ref, og_ref, os_ref, om_ref, w_o_ref, g_ref, b_ref, w_in_ref, w_out_ref,
                 p_ref, w_g_ref, w_p_ref, out_ref):
    n_g = GDN_HEADS * GDN_DV
    n_s = SB_HEADS * SB_DIM
    x = h_ref[...]
    mix = (_dot(og_ref[...], w_o_ref[:n_g, :]) + _dot(os_ref[...], w_o_ref[n_g:n_g + n_s, :])
           + _dot(om_ref[...], w_o_ref[n_g + n_s:, :]))
    h1 = _layer_norm(DEEPNORM_ALPHA * x + mix, g_ref[1:2, :], b_ref[1:2, :])
    y = _swiglu(h1.astype(BF16), w_in_ref, w_out_ref)
    h2 = _layer_norm(DEEPNORM_ALPHA * h1 + 0.5 * y, g_ref[2:3, :], b_ref[2:3, :])
    gate = _sigmoid(_dot(h2.astype(BF16), w_g_ref[...]))
    out_ref[...] = h2 + gate * _dot(p_ref[...].astype(BF16), w_p_ref[...])


def _post(layer, h, o_gdn, o_sb, o_mla, w_o, ln_g, ln_b, w_in, w_out, p, w_g, w_p):
    t = h.shape[0]
    tm = TOKEN_TILE
    row = lambda n: pl.BlockSpec((tm, n), lambda i: (i, 0))
    return pl.pallas_call(
        _post_kernel,
        grid=(t // tm,),
        in_specs=[row(D_MODEL), row(o_gdn.shape[1]), row(o_sb.shape[1]), row(o_mla.shape[1]),
                  _of_layer(w_o, layer), _of_layer(ln_g, layer), _of_layer(ln_b, layer),
                  _of_layer(w_in, layer), _of_layer(w_out, layer),
                  pl.BlockSpec((None, tm, PLE_DIM), lambda i: (layer, i, 0)),
                  _of_layer(w_g, layer), _of_layer(w_p, layer)],
        out_specs=row(D_MODEL),
        out_shape=jax.ShapeDtypeStruct((t, D_MODEL), F32),
        compiler_params=_params(1),
        name="post",
    )(h, o_gdn, o_sb, o_mla, w_o, ln_g, ln_b, w_in, w_out, p, w_g, w_p)


def _zeros_like_cols(w, n):
    return jnp.zeros(w.shape[:-1] + (n,), w.dtype)


def _regroup_mix(w):
    edges = [N_QKV + N_Z, GDN_HEADS, GDN_HEADS, N_SB + MLA_Q_RANK + MLA_KV_RANK, MLA_ROPE]
    parts, o = [], 0
    for n in edges:
        parts.append(w[..., o:o + n])
        o += n
    gdn, ga, gb, att, kr = parts
    half = MLA_ROPE // 2
    pad = LANES - MLA_NOPE - MLA_ROPE
    m1 = [ga, gb, _zeros_like_cols(w, MLA_NOPE - 2 * GDN_HEADS), kr, _zeros_like_cols(w, pad)]
    m2 = [_zeros_like_cols(w, MLA_NOPE), kr[..., half:], kr[..., :half], _zeros_like_cols(w, pad)]
    return gdn.astype(BF16), att.astype(BF16), jnp.concatenate(m1 + m2, axis=-1).astype(BF16)


def _regroup_uq(w):
    d = MLA_NOPE + MLA_ROPE
    half = MLA_ROPE // 2
    pad = _zeros_like_cols(w, LANES - d)
    main, swap = [], []
    for h in range(MLA_HEADS):
        main += [w[..., h * d:(h + 1) * d], pad]
        swap += [_zeros_like_cols(w, MLA_NOPE), w[..., h * d + MLA_NOPE + half:(h + 1) * d],
                 w[..., h * d + MLA_NOPE:h * d + MLA_NOPE + half], pad]
    return jnp.concatenate(main + swap, axis=-1)


def _regroup_ukv(w):
    d = MLA_NOPE + MLA_V
    keys, vals = [], []
    for h in range(MLA_HEADS):
        keys += [w[..., h * d:h * d + MLA_NOPE], _zeros_like_cols(w, LANES - MLA_NOPE)]
        vals.append(w[..., h * d + MLA_NOPE:(h + 1) * d])
    return jnp.concatenate(keys + vals, axis=-1)


def _lane_rows(vals, start):
    return jnp.pad(vals.astype(F32), ((0, 0), (start, LANES - start - vals.shape[1])))


def _rope_rows():
    inv = 1.0 / (ROPE_BASE ** (jnp.arange(0, MLA_ROPE, 2, dtype=F32) / MLA_ROPE))
    half = MLA_ROPE // 2
    pad = LANES - MLA_NOPE - MLA_ROPE
    z = lambda n: jnp.zeros((n,), F32)
    one = lambda n: jnp.ones((n,), F32)
    freq = jnp.concatenate([z(MLA_NOPE), inv, inv, z(pad)])
    cos_mask = jnp.concatenate([one(MLA_NOPE + MLA_ROPE), z(pad)])
    sin_sign = jnp.concatenate([z(MLA_NOPE), -one(half), one(half), z(pad)])
    rows = jnp.stack([freq, cos_mask, sin_sign])
    return jnp.pad(rows, ((0, ROPE_ROWS - rows.shape[0]), (0, 0)))


def kernel(x, p, positions, ffa_w_in, ffa_w_out, mix_w_in, gdn_conv_w, gdn_a_log, gdn_dt_bias, gdn_norm_w, mla_q_norm_w, mla_kv_norm_w, mla_w_uq, mla_w_ukv, mix_w_o, ffb_w_in, ffb_w_out, ln_g, ln_b, ple_w_gate, ple_w_proj):
    batch, seq, d = x.shape
    t = batch * seq
    depth = p.shape[0]
    hd = GDN_HEADS * GDN_DK

    bf = lambda w: w.astype(BF16)
    w_mix, w_uq, w_ukv = _regroup_mix(mix_w_in), bf(_regroup_uq(mla_w_uq)), bf(_regroup_ukv(mla_w_ukv))
    ffa_in, ffa_out, ffb_in, ffb_out = bf(ffa_w_in), bf(ffa_w_out), bf(ffb_w_in), bf(ffb_w_out)
    w_o, w_g, w_p = bf(mix_w_o), bf(ple_w_gate), bf(ple_w_proj)
    alog, dtb = _lane_rows(gdn_a_log, A_LANE), _lane_rows(gdn_dt_bias, A_LANE)
    nw = jnp.concatenate([gdn_norm_w, gdn_norm_w], axis=-1).astype(F32)
    rope = _rope_rows()
    pos = positions.reshape(t, 1).astype(F32)
    p_rows = p.reshape(depth, t, p.shape[-1])

    h = x.reshape(t, d)
    for i in range(depth):
        h, qkv, z, sb, m1, q_m, k_m, v_m = _ffn_inproj(i, h, pos, rope, ffa_in, ffa_out, ln_g, ln_b, w_mix,
                                                       mla_q_norm_w, mla_kv_norm_w, w_uq, w_ukv)
        o_gdn = _gdn(i, qkv, z, m1, gdn_conv_w, alog, dtb, nw, batch)
        o_sb = _sb_attn(sb, batch)
        o_mla = _mla_attn(q_m, k_m, v_m, batch)
        h = _post(i, h, o_gdn, o_sb, o_mla, w_o, ln_g, ln_b, ffb_in, ffb_out, p_rows, w_g, w_p)
    return h.reshape(batch, seq, d)
```

```python
import functools

import jax
import jax.numpy as jnp
from jax import lax
from jax.experimental import pallas as pl
from jax.experimental.pallas import tpu as pltpu

F32 = jnp.float32
BF16 = jnp.bfloat16

DEPTH = 2
D_MODEL = 1024
PLE_DIM = 256
D_FF = 2816
LN_EPS = 1e-5
RMS_EPS = 1e-6
DEEPNORM_ALPHA = (2 * DEPTH) ** 0.25

GDN_HEADS = 8
GDN_DK = 64
GDN_DV = 64
GDN_CONV = 4
SB_HEADS = 4
SB_DIM = 64
MLA_HEADS = 4
MLA_NOPE = 64
MLA_ROPE = 32
MLA_V = 64
MLA_Q_RANK = 256
MLA_KV_RANK = 128
ROPE_BASE = 10000.0

LOG2E = 1.4426950408889634
LANES = 128
HALF = LANES // 2
VMEM_LIMIT = 56 * 1024 * 1024

GDN_CHUNK = 128
GDN_CHUNKS_PER_STEP = 4
GDN_GROUP = 2
SUB_TILE = 256
TOKEN_TILE = 512
POST_TILE = 512
SB_QUERY_BLOCK = 512
SB_STEP_GROUPS = 2
SB_DEAD_LOG2 = 150.0
MLA_BLOCK = 1024
MLA_KEY_BLOCK = 1024

N_HD = GDN_HEADS * GDN_DK
N_QKV = 3 * N_HD
N_Z = GDN_HEADS * GDN_DV
N_SB = 3 * SB_HEADS * SB_DIM
N_ML = MLA_Q_RANK + MLA_KV_RANK + 2 * LANES
A_LANE = 0
B_LANE = GDN_HEADS
N_MIX = N_QKV + N_Z + N_SB + N_ML
N_MLA_QK = MLA_HEADS * LANES
N_MLA_V = MLA_HEADS * MLA_V
ROPE_ROWS = 8


def _sigmoid(x):
    return 0.5 * jnp.tanh(0.5 * x) + 0.5


def _silu(x):
    half = 0.5 * x
    return half * jnp.tanh(half) + half


def _softplus(x):
    return jnp.maximum(x, 0.0) + jnp.log1p(jnp.exp(-jnp.abs(x)))


def _layer_norm(r, g, b):
    mu = jnp.mean(r, axis=-1, keepdims=True)
    d = r - mu
    var = jnp.mean(d * d, axis=-1, keepdims=True)
    return d * lax.rsqrt(var + LN_EPS) * g + b


def _rms_norm(x, w):
    return x * lax.rsqrt(jnp.mean(x * x, axis=-1, keepdims=True) + RMS_EPS) * w


def _dot(a, b):
    return jnp.dot(a, b, preferred_element_type=F32)


def _dot_nt(a, b):
    return lax.dot_general(a, b, (((1,), (1,)), ((), ())), preferred_element_type=F32)


def _head_rows(x, lo):
    zero = jnp.zeros_like(x)
    return jnp.concatenate([jnp.where(lo, x, zero), jnp.where(lo, zero, x)], axis=0)


def _replace_tail(full, r0, tail):
    return tail if r0 == 0 else jnp.concatenate([full[:r0], tail], axis=0)


def _split2(x):
    hi = x.astype(BF16)
    lo = (x - hi.astype(F32)).astype(BF16)
    return hi, lo


def _split3(x):
    hi = x.astype(BF16)
    r = x - hi.astype(F32)
    mid = r.astype(BF16)
    lo = (r - mid.astype(F32)).astype(BF16)
    return hi, mid, lo


def _swiglu(xb, w_in_ref, w_out_ref):
    gu = _dot(xb, w_in_ref[...])
    act = _silu(gu[:, :D_FF]) * gu[:, D_FF:]
    return _dot(act.astype(BF16), w_out_ref[...])


def _sub_tiles(rows):
    return [slice(s, s + SUB_TILE) for s in range(0, rows, SUB_TILE)]


def _round_robin(gens):
    live = list(gens)
    while live:
        for g in list(live):
            try:
                next(g)
            except StopIteration:
                live.remove(g)


def _whole(a):
    return pl.BlockSpec(a.shape, lambda *_: (0,) * a.ndim, pipeline_mode=pl.Buffered(1))


def _of_layer(a, layer, block=None, at=None):
    tail = tuple(a.shape[1:]) if block is None else tuple(block)
    idx = (0,) * len(tail) if at is None else tuple(at)
    return pl.BlockSpec((None,) + tail, lambda *_: (layer,) + idx, pipeline_mode=pl.Buffered(1))


def _params(n_axes):
    return pltpu.CompilerParams(dimension_semantics=("arbitrary",) * n_axes, vmem_limit_bytes=VMEM_LIMIT)


def _ffn_inproj_kernel(layer, h_ref, pos_ref, rope_ref, w_in_ref, w_out_ref, g_ref, b_ref,
                       w_gdn_ref, w_att_ref, w_m_ref, qnw_ref, kvnw_ref, w_uq_ref, w_ukv_ref,
                       h_out, qkv_out, z_out, sb_out, m1_out, mq_out, mk_out, mv_out):
    lane = lax.broadcasted_iota(jnp.int32, (1, LANES), 1)
    rot = (lane >= MLA_NOPE) & (lane < MLA_NOPE + MLA_ROPE)
    n_q = SB_HEADS * SB_DIM
    scale = (MLA_NOPE + MLA_ROPE) ** -0.5 * LOG2E

    def rows_of(r):
        x = h_ref[r, :]
        gu = _dot(x.astype(BF16), w_in_ref[...])
        yield
        y = _dot((_silu(gu[:, :D_FF]) * gu[:, D_FF:]).astype(BF16), w_out_ref[...])
        ang = pos_ref[r, :] * rope_ref[0:1, :]
        cos = jnp.cos(ang) * rope_ref[1:2, :]
        sin = jnp.sin(ang) * rope_ref[2:3, :]
        yield
        hn = _layer_norm(DEEPNORM_ALPHA * x + 0.5 * y, g_ref[0:1, :], b_ref[0:1, :])
        h_out[r, :] = hn
        hb = hn.astype(BF16)
        gdn = _dot(hb, w_gdn_ref[...])
        att = _dot(hb, w_att_ref[...])
        m12 = _dot(hb, w_m_ref[...])
        yield
        qkv_out[r, :] = gdn[:, :N_QKV]
        z_out[r, :] = gdn[:, N_QKV:]
        sb_out[r, :n_q] = (att[:, :n_q] * (SB_DIM ** -0.5 * LOG2E)).astype(BF16)
        sb_out[r, n_q:] = att[:, n_q:N_SB].astype(BF16)
        mq = att[:, N_SB:N_SB + MLA_Q_RANK]
        ckv = att[:, N_SB + MLA_Q_RANK:]
        m1 = m12[:, :LANES]
        m2 = m12[:, LANES:]
        m1_out[r, :] = m1
        qf = _dot(_rms_norm(mq, qnw_ref[layer:layer + 1, :]).astype(BF16), w_uq_ref[...])
        kv = _dot(_rms_norm(ckv, kvnw_ref[layer:layer + 1, :]).astype(BF16), w_ukv_ref[...])
        yield
        k_rot = jnp.where(rot, m1 * cos, 0.0) + m2 * sin
        for h in range(MLA_HEADS):
            sl = slice(h * LANES, (h + 1) * LANES)
            q_h = qf[:, sl] * cos + qf[:, N_MLA_QK + h * LANES:N_MLA_QK + (h + 1) * LANES] * sin
            mq_out[r, sl] = (q_h * scale).astype(BF16)
            mk_out[r, sl] = (kv[:, sl] + k_rot).astype(BF16)
        mv_out[r, :] = kv[:, N_MLA_QK:].astype(BF16)

    _round_robin([rows_of(r) for r in _sub_tiles(h_ref.shape[0])])


def _ffn_inproj(layer, h, pos, rope, w_in, w_out, ln_g, ln_b, w_mix, qnw, kvnw, w_uq, w_ukv):
    w_gdn, w_att, w_m = w_mix
    t = h.shape[0]
    tm = TOKEN_TILE
    row = lambda n: pl.BlockSpec((tm, n), lambda i: (i, 0))
    widths = (D_MODEL, N_QKV, N_Z, N_SB, LANES, N_MLA_QK, N_MLA_QK, N_MLA_V)
    dtypes = (F32, F32, F32, BF16, F32, BF16, BF16, BF16)
    return pl.pallas_call(
        functools.partial(_ffn_inproj_kernel, layer),
        grid=(t // tm,),
        in_specs=[row(D_MODEL), row(1), _whole(rope), _of_layer(w_in, layer), _of_layer(w_out, layer),
                  _of_layer(ln_g, layer), _of_layer(ln_b, layer),
                  _of_layer(w_gdn, layer), _of_layer(w_att, layer), _of_layer(w_m, layer),
                  _whole(qnw), _whole(kvnw), _of_layer(w_uq, layer), _of_layer(w_ukv, layer)],
        out_specs=[row(n) for n in widths],
        out_shape=[jax.ShapeDtypeStruct((t, n), dt) for n, dt in zip(widths, dtypes)],
        compiler_params=_params(1),
        name="ffn_inproj",
    )(h, pos, rope, w_in, w_out, ln_g, ln_b, w_gdn, w_att, w_m, qnw, kvnw, w_uq, w_ukv)


def _unit_lower_inverses(l_strict, row, col):
    c = l_strict[0].shape[0]
    blk = lambda v, log2: jnp.right_shift(v, log2)
    pair = (blk(row, 1) == blk(col, 1)) & (row > col)
    eye = jnp.where(row == col, 1.0, 0.0)
    xs = [eye - jnp.where(pair, l, 0.0) for l in l_strict]
    log2 = 1
    while (2 << log2) <= c:
        off = (blk(row, log2 + 1) == blk(col, log2 + 1)) & (blk(row, log2) > blk(col, log2))
        xb = [x.astype(BF16) for x in xs]
        lx = [_dot(jnp.where(off, l, 0.0).astype(BF16), b).astype(BF16) for l, b in zip(l_strict, xb)]
        yield
        xs = [x - _dot(b, t) for x, b, t in zip(xs, xb, lx)]
        yield
        log2 += 1
    return xs


def _interleave(main, side):
    values, live = [None, None], [main, side]
    while any(g is not None for g in live):
        for n, g in enumerate(live):
            if g is not None:
                try:
                    next(g)
                except StopIteration as stop:
                    values[n], live[n] = stop.value, None
    return values


def _dot_exact(a, b, terms):
    split = {2: _split2, 3: _split3}[terms]
    if a.dtype == F32:
        parts = [_dot(piece, b) for piece in split(a)]
    else:
        parts = [_dot(a, piece) for piece in split(b)]
    return functools.reduce(lambda x, y: x + y, parts)


def _gdn_kernel(layer, qkv_ref, z_ref, m1_ref, cw_ref, alog_ref, dtb_ref, nw_ref, o_ref, buf, state_ref):
    c = GDN_CHUNK
    chunks = qkv_ref.shape[0] // c
    pairs = GDN_HEADS // 2
    first = pl.program_id(1) == 0

    @pl.when(first)
    def _():
        state_ref[...] = jnp.zeros(state_ref.shape, F32)

    lane = lax.broadcasted_iota(jnp.int32, (1, LANES), 1)
    lo = lane < HALF
    row = lax.broadcasted_iota(jnp.int32, (c, c), 0)
    col = lax.broadcasted_iota(jnp.int32, (c, c), 1)
    incl = row >= col
    strict = row > col
    tril = jnp.where(incl, 1.0, 0.0).astype(BF16)
    same_head = jnp.where(jnp.right_shift(row, 6) == jnp.right_shift(col, 6), 1.0, 0.0).astype(BF16)
    wide_row = lax.broadcasted_iota(jnp.int32, (LANES, GDN_HEADS * LANES), 0)
    wide_head = jnp.right_shift(lax.broadcasted_iota(jnp.int32, (LANES, GDN_HEADS * LANES), 1), 7)
    pick_b = jnp.where(wide_row == wide_head + B_LANE, 1.0, 0.0).astype(BF16)
    a_row = -jnp.exp(alog_ref[layer:layer + 1, :])
    dt_row = dtb_ref[layer:layer + 1, :]
    nw = nw_ref[layer:layer + 1, :]
    heads = range(GDN_HEADS)
    hs = [slice(h * LANES, (h + 1) * LANES) for h in heads]
    ps = [slice(p * LANES, (p + 1) * LANES) for p in range(pairs)]

    @pl.when(first)
    def _():
        buf[...] = jnp.zeros(buf.shape, F32)


    def prepare(j):
        rows = slice(j * c, (j + 1) * c)

        def conv_silu(group):
            sl = slice(group * LANES, (group + 1) * LANES)
            x = qkv_ref[rows, sl]
            before = buf[:, sl] if j == 0 else qkv_ref[j * c - 8:j * c, sl]
            xx = jnp.concatenate([before, x], axis=0)
            y = x * cw_ref[GDN_CONV - 1:GDN_CONV, sl]
            for tap in range(GDN_CONV - 1):
                y = y + pltpu.roll(xx, GDN_CONV - 1 - tap, axis=0)[8:] * cw_ref[tap:tap + 1, sl]
            return _silu(y)

        m1 = m1_ref[rows, :]
        gc = _dot_exact(tril, a_row * _softplus(m1 + dt_row), 3)
        gc_rows = gc.T
        beta_wide = _dot_exact(_sigmoid(m1), pick_b, 2)
        yield
        out = []
        for p in range(pairs):
            xq, xk, xv = conv_silu(p), conv_silu(pairs + p), conv_silu(2 * pairs + p)
            yield
            qn = xq * (lax.rsqrt(_dot_exact(xq * xq, same_head, 2) + RMS_EPS) * (GDN_DK ** -0.5))
            kn = xk * lax.rsqrt(_dot_exact(xk * xk, same_head, 2) + RMS_EPS)
            yield
            for h in (2 * p, 2 * p + 1):
                mine = lo if h % 2 == 0 else jnp.logical_not(lo)
                gcb = jnp.broadcast_to(gc[:, A_LANE + h:A_LANE + h + 1], (c, LANES))
                gc_row = gc_rows[A_LANE + h:A_LANE + h + 1, :]
                out.append(dict(
                    q=jnp.where(mine, qn, 0.0), k=jnp.where(mine, kn, 0.0), v=jnp.where(mine, xv, 0.0),
                    gcb=gcb, beta=beta_wide[:, hs[h]],
                    decay=jnp.where(incl, jnp.exp(jnp.where(incl, gcb - gc_row, 0.0)), 0.0), e=jnp.exp(gcb)))
                yield
        return out

    def solve(pre):
        gram = [_dot_nt(jnp.concatenate([t["q"], t["k"]], axis=0).astype(BF16), t["k"].astype(BF16))
                for t in pre]
        yield
        l_strict = [jnp.where(strict, g[c:] * t["decay"] * t["beta"], 0.0) for g, t in zip(gram, pre)]
        t_inv = yield from _unit_lower_inverses(l_strict, row, col)
        out = []
        for g, t, ti in zip(gram, pre, t_inv):
            uw = _dot(ti.astype(BF16), jnp.concatenate([t["v"] * t["beta"], t["k"] * (t["beta"] * t["e"])],
                                                       axis=1).astype(BF16))
            g_end = t["gcb"][c - 1:c, :]
            out.append(dict(
                u=uw[:, :LANES], wq=jnp.concatenate([uw[:, LANES:], t["q"] * t["e"]], axis=0).astype(BF16),
                qk_kd=jnp.concatenate([g[:c] * t["decay"], (t["k"] * jnp.exp(g_end - t["gcb"])).T],
                                      axis=0).astype(BF16),
                keep=jnp.exp(g_end)))
        yield
        return out

    def advance(j, sol, state):
        ws = [_dot(s["wq"], st.astype(BF16)) for s, st in zip(sol, state)]
        yield
        fin = [_dot(s["qk_kd"], (s["u"] - w[:c]).astype(BF16)) for s, w in zip(sol, ws)]
        yield
        new_state = [st * s["keep"] + f[c:] for st, s, f in zip(state, sol, fin)]
        rows = slice(j * c, (j + 1) * c)
        for p in range(pairs):
            o_pair = (ws[2 * p][c:] + fin[2 * p][:c]) + (ws[2 * p + 1][c:] + fin[2 * p + 1][:c])
            ms = _dot_exact(o_pair * o_pair, same_head, 2) * (1.0 / GDN_DV)
            o_ref[rows, ps[p]] = (o_pair * lax.rsqrt(ms + RMS_EPS) * nw * _silu(z_ref[rows, ps[p]])).astype(o_ref.dtype)
            yield
        return new_state

    def prepare_group(js):
        out = []
        for j in js:
            out += yield from prepare(j)
        return out

    def group_matmuls(js, pre, state):
        sol = yield from solve(pre)
        for n, j in enumerate(js):
            state = yield from advance(j, sol[n * GDN_HEADS:(n + 1) * GDN_HEADS], state)
        return state

    def nothing():
        return None
        yield

    groups = [list(range(g0, min(g0 + GDN_GROUP, chunks))) for g0 in range(0, chunks, GDN_GROUP)]
    state = [state_ref[h] for h in heads]
    _, pre = _interleave(nothing(), prepare_group(groups[0]))
    for n, js in enumerate(groups):
        upcoming = prepare_group(groups[n + 1]) if n + 1 < len(groups) else nothing()
        state, pre = _interleave(group_matmuls(js, pre, state), upcoming)
    for h in heads:
        state_ref[h] = state[h]
    buf[...] = qkv_ref[chunks * c - 8:chunks * c, :]


def _gdn(layer, qkv, z, m1, conv_w, alog, dtb, nw, batch):
    t = qkv.shape[0]
    c = GDN_CHUNK * GDN_CHUNKS_PER_STEP
    nc = t // batch // c
    blk = lambda n: pl.BlockSpec((c, n), lambda b, i: (b * nc + i, 0))
    return pl.pallas_call(
        functools.partial(_gdn_kernel, layer),
        grid=(batch, nc),
        in_specs=[blk(N_QKV), blk(N_Z), blk(LANES), _of_layer(conv_w, layer),
                  _whole(alog), _whole(dtb), _whole(nw)],
        out_specs=blk(GDN_HEADS * GDN_DV),
        out_shape=jax.ShapeDtypeStruct((t, GDN_HEADS * GDN_DV), BF16),
        scratch_shapes=[pltpu.VMEM((8, N_QKV), F32), pltpu.VMEM((GDN_HEADS, LANES, LANES), F32)],
        compiler_params=_params(2),
        name="gdn",
    )(qkv, z, m1, conv_w, alog, dtb, nw)


def _sb_kernel(q_ref, k_ref, v_ref, o_ref):
    bq = q_ref.shape[0]
    g = LANES
    per = bq // g
    i = pl.program_id(2)
    lane = lax.broadcasted_iota(jnp.int32, (1, LANES), 1)
    lo = lane < HALF
    q = q_ref[...]
    row = lax.broadcasted_iota(jnp.int32, (bq, g), 0)
    col = lax.broadcasted_iota(jnp.int32, (bq, g), 1)
    jr = lax.broadcasted_iota(jnp.int32, (g, g), 0)
    jc = lax.broadcasted_iota(jnp.int32, (g, g), 1)
    later = jnp.where(jr > jc, 1.0, 0.0).astype(BF16)
    suffix_total = jnp.concatenate([later, jnp.ones((g, g), BF16)], axis=1)
    suffix_total = jnp.concatenate([suffix_total, suffix_total], axis=0)

    def group(start, run, acc, mask, r0):
        k_g = k_ref[pl.ds(start, g), :]
        v_g = v_ref[pl.ds(start, g), :]
        z = _dot_nt(q[r0:], _head_rows(k_g, lo))
        nl = jnp.maximum(z, 0.0) + jnp.log(1.0 + jnp.exp2(-jnp.abs(z))) * LOG2E
        if mask is not None:
            nl = jnp.where(mask, nl, 0.0)
        wts, new_run = [], []
        for hh in range(2):
            sl = slice(hh * g, (hh + 1) * g)
            hi, low = _split2(nl[:, sl])
            st = _dot(jnp.concatenate([hi, low], axis=1), suffix_total)
            wts.append(jnp.exp2(z[:, sl] - nl[:, sl] - (run[hh][r0:] + st[:, :g])))
            new_run.append(_replace_tail(run[hh], r0, run[hh][r0:] + st[:, g:]))
        wts = jnp.concatenate(wts, axis=1)
        if mask is not None:
            wts = jnp.where(mask, wts, 0.0)
        pv = _dot(wts.astype(BF16), _head_rows(v_g, lo))
        return tuple(new_run), _replace_tail(acc, r0, acc[r0:] + pv)

    run = (jnp.zeros((bq, g), F32), jnp.zeros((bq, g), F32))
    acc = jnp.zeros((bq, LANES), F32)
    base = i * bq
    for d in range(per):
        off = (per - 1 - d) * g
        mask = ((col + off) < row)[off:]
        run, acc = group(pl.multiple_of(base + off, g), run, acc, jnp.concatenate([mask, mask], axis=1), off)

    step = SB_STEP_GROUPS * g
    steps = i * (bq // step)

    def lowest(run):
        return jnp.min(jnp.minimum(run[0], run[1]))

    def more(carry):
        n, low, _, _ = carry
        return jnp.logical_and(n < steps, low < SB_DEAD_LOG2)

    def walk(carry):
        n, _, run, acc = carry
        start = base - (n + 1) * step
        for d in range(SB_STEP_GROUPS):
            run, acc = group(pl.multiple_of(start + (SB_STEP_GROUPS - 1 - d) * g, g), run, acc, None, 0)
        return n + 1, lowest(run), run, acc

    _, _, run, acc = lax.while_loop(more, walk, (jnp.int32(0), lowest(run), run, acc))
    o_ref[...] = acc.astype(o_ref.dtype)


def _sb_attn(sb, batch):
    t = sb.shape[0]
    s = t // batch
    bq = SB_QUERY_BLOCK
    nq = s // bq
    pairs = SB_HEADS // 2
    return pl.pallas_call(
        _sb_kernel,
        grid=(batch, pairs, nq),
        in_specs=[pl.BlockSpec((bq, LANES), lambda b, p, i: (b * nq + i, p)),
                  pl.BlockSpec((s, LANES), lambda b, p, i: (b, pairs + p)),
                  pl.BlockSpec((s, LANES), lambda b, p, i: (b, 2 * pairs + p))],
        out_specs=pl.BlockSpec((bq, LANES), lambda b, p, i: (b * nq + i, p)),
        out_shape=jax.ShapeDtypeStruct((t, SB_HEADS * SB_DIM), BF16),
        compiler_params=_params(3),
        name="sb_attn",
    )(sb, sb, sb)


def _mla_kernel(q_ref, k_ref, v_ref, o_ref):
    bq = q_ref.shape[0]
    bk = MLA_KEY_BLOCK
    per = bq // bk
    i = pl.program_id(2)
    lane = lax.broadcasted_iota(jnp.int32, (1, LANES), 1)
    lo = lane < HALF
    q = q_ref[...]
    row = lax.broadcasted_iota(jnp.int32, (bq, bk), 0)
    col = lax.broadcasted_iota(jnp.int32, (bq, bk), 1)

    def visit(start, carry, mask, r0):
        stats, acc = carry
        k_b = k_ref[pl.ds(start, bk), :]
        v_b = v_ref[pl.ds(start, bk), :]
        new, probs, alphas = [], [], []
        for hh in range(2):
            m, l = stats[hh]
            sl = slice(hh * LANES, (hh + 1) * LANES)
            s = _dot_nt(q[r0:, sl], k_b[:, sl])
            if mask is not None:
                s = jnp.where(mask, s, -jnp.inf)
            m_new = jnp.maximum(m[r0:], jnp.max(s, axis=-1, keepdims=True))
            alpha = jnp.exp2(m[r0:] - m_new)
            p = jnp.exp2(s - m_new)
            new.append((_replace_tail(m, r0, m_new),
                        _replace_tail(l, r0, alpha * l[r0:] + jnp.sum(p, axis=-1, keepdims=True))))
            probs.append(p.astype(BF16))
            alphas.append(alpha)
        pv = _dot(jnp.concatenate(probs, axis=1), _head_rows(v_b, lo))
        return tuple(new), _replace_tail(acc, r0, jnp.where(lo, alphas[0], alphas[1]) * acc[r0:] + pv)

    carry = (tuple((jnp.full((bq, 1), -jnp.inf, F32), jnp.zeros((bq, 1), F32)) for _ in range(2)),
             jnp.zeros((bq, LANES), F32))
    base = i * bq
    for d in range(per):
        mask = ((col + d * bk) <= row)[d * bk:]
        carry = visit(pl.multiple_of(base + d * bk, bk), carry, mask, d * bk)

    def block(n, cr):
        for d in range(per):
            cr = visit(pl.multiple_of(n * bq + d * bk, bk), cr, None, 0)
        return cr

    stats, acc = lax.fori_loop(0, i, block, carry)
    o_ref[...] = (acc / jnp.where(lo, stats[0][1], stats[1][1])).astype(o_ref.dtype)


def _mla_attn(q, k, v, batch):
    t = q.shape[0]
    s = t // batch
    bq = MLA_BLOCK
    nq = s // bq
    pairs = MLA_HEADS // 2
    return pl.pallas_call(
        _mla_kernel,
        grid=(batch, pairs, nq),
        in_specs=[pl.BlockSpec((bq, 2 * LANES), lambda b, p, i: (b * nq + i, p)),
                  pl.BlockSpec((s, 2 * LANES), lambda b, p, i: (b, p)),
                  pl.BlockSpec((s, LANES), lambda b, p, i: (b, p))],
        out_specs=pl.BlockSpec((bq, LANES), lambda b, p, i: (b * nq + i, p)),
        out_shape=jax.ShapeDtypeStruct((t, MLA_HEADS * MLA_V), BF16),
        compiler_params=_params(3),
        name="mla_attn",
    )(q, k, v)


def _post_kernel(h_ref, og_ref, os_ref, om_ref, w_o_ref, g_ref, b_ref, w_in_ref, w_out_ref,
                 p_ref, w_g_ref, w_p_ref, out_ref):
    n_g = GDN_HEADS * GDN_DV
    n_s = SB_HEADS * SB_DIM

    def rows_of(r):
        x = h_ref[r, :]
        mix = (_dot(og_ref[r, :], w_o_ref[:n_g, :]) + _dot(os_ref[r, :], w_o_ref[n_g:n_g + n_s, :])
               + _dot(om_ref[r, :], w_o_ref[n_g + n_s:, :]))
        yield
        h1 = _layer_norm(DEEPNORM_ALPHA * x + mix, g_ref[1:2, :], b_ref[1:2, :])
        gu = _dot(h1.astype(BF16), w_in_ref[...])
        yield
        y = _dot((_silu(gu[:, :D_FF]) * gu[:, D_FF:]).astype(BF16), w_out_ref[...])
        yield
        h2 = _layer_norm(DEEPNORM_ALPHA * h1 + 0.5 * y, g_ref[2:3, :], b_ref[2:3, :])
        gate = _dot(h2.astype(BF16), w_g_ref[...])
        emb = _dot(p_ref[r, :].astype(BF16), w_p_ref[...])
        yield
        out_ref[r, :] = h2 + _sigmoid(gate) * emb

    _round_robin([rows_of(r) for r in _sub_tiles(h_ref.shape[0])])


def _post(layer, h, o_gdn, o_sb, o_mla, w_o, ln_g, ln_b, w_in, w_out, p, w_g, w_p):
    t = h.shape[0]
    tm = POST_TILE
    row = lambda n: pl.BlockSpec((tm, n), lambda i: (i, 0))
    return pl.pallas_call(
        _post_kernel,
        grid=(t // tm,),
        in_specs=[row(D_MODEL), row(o_gdn.shape[1]), row(o_sb.shape[1]), row(o_mla.shape[1]),
                  _of_layer(w_o, layer), _of_layer(ln_g, layer), _of_layer(ln_b, layer),
                  _of_layer(w_in, layer), _of_layer(w_out, layer),
                  pl.BlockSpec((None, tm, PLE_DIM), lambda i: (layer, i, 0)),
                  _of_layer(w_g, layer), _of_layer(w_p, layer)],
        out_specs=row(D_MODEL),
        out_shape=jax.ShapeDtypeStruct((t, D_MODEL), F32),
        compiler_params=_params(1),
        name="post",
    )(h, o_gdn, o_sb, o_mla, w_o, ln_g, ln_b, w_in, w_out, p, w_g, w_p)


def _zeros_like_cols(w, n):
    return jnp.zeros(w.shape[:-1] + (n,), w.dtype)


def _regroup_mix(w):
    edges = [N_QKV + N_Z, GDN_HEADS, GDN_HEADS, N_SB + MLA_Q_RANK + MLA_KV_RANK, MLA_ROPE]
    parts, o = [], 0
    for n in edges:
        parts.append(w[..., o:o + n])
        o += n
    gdn, ga, gb, att, kr = parts
    half = MLA_ROPE // 2
    pad = LANES - MLA_NOPE - MLA_ROPE
    m1 = [ga, gb, _zeros_like_cols(w, MLA_NOPE - 2 * GDN_HEADS), kr, _zeros_like_cols(w, pad)]
    m2 = [_zeros_like_cols(w, MLA_NOPE), kr[..., half:], kr[..., :half], _zeros_like_cols(w, pad)]
    return gdn.astype(BF16), att.astype(BF16), jnp.concatenate(m1 + m2, axis=-1).astype(BF16)


def _regroup_uq(w):
    d = MLA_NOPE + MLA_ROPE
    half = MLA_ROPE // 2
    pad = _zeros_like_cols(w, LANES - d)
    main, swap = [], []
    for h in range(MLA_HEADS):
        main += [w[..., h * d:(h + 1) * d], pad]
        swap += [_zeros_like_cols(w, MLA_NOPE), w[..., h * d + MLA_NOPE + half:(h + 1) * d],
                 w[..., h * d + MLA_NOPE:h * d + MLA_NOPE + half], pad]
    return jnp.concatenate(main + swap, axis=-1)


def _regroup_ukv(w):
    d = MLA_NOPE + MLA_V
    keys, vals = [], []
    for h in range(MLA_HEADS):
        keys += [w[..., h * d:h * d + MLA_NOPE], _zeros_like_cols(w, LANES - MLA_NOPE)]
        vals.append(w[..., h * d + MLA_NOPE:(h + 1) * d])
    return jnp.concatenate(keys + vals, axis=-1)


def _lane_rows(vals, start):
    return jnp.pad(vals.astype(F32), ((0, 0), (start, LANES - start - vals.shape[1])))


def _rope_rows():
    inv = 1.0 / (ROPE_BASE ** (jnp.arange(0, MLA_ROPE, 2, dtype=F32) / MLA_ROPE))
    half = MLA_ROPE // 2
    pad = LANES - MLA_NOPE - MLA_ROPE
    z = lambda n: jnp.zeros((n,), F32)
    one = lambda n: jnp.ones((n,), F32)
    freq = jnp.concatenate([z(MLA_NOPE), inv, inv, z(pad)])
    cos_mask = jnp.concatenate([one(MLA_NOPE + MLA_ROPE), z(pad)])
    sin_sign = jnp.concatenate([z(MLA_NOPE), -one(half), one(half), z(pad)])
    rows = jnp.stack([freq, cos_mask, sin_sign])
    return jnp.pad(rows, ((0, ROPE_ROWS - rows.shape[0]), (0, 0)))


def kernel(x, p, positions, ffa_w_in, ffa_w_out, mix_w_in, gdn_conv_w, gdn_a_log, gdn_dt_bias, gdn_norm_w, mla_q_norm_w, mla_kv_norm_w, mla_w_uq, mla_w_ukv, mix_w_o, ffb_w_in, ffb_w_out, ln_g, ln_b, ple_w_gate, ple_w_proj):
    batch, seq, d = x.shape
    t = batch * seq
    depth = p.shape[0]
    hd = GDN_HEADS * GDN_DK

    bf = lambda w: w.astype(BF16)
    w_mix, w_uq, w_ukv = _regroup_mix(mix_w_in), bf(_regroup_uq(mla_w_uq)), bf(_regroup_ukv(mla_w_ukv))
    ffa_in, ffa_out, ffb_in, ffb_out = bf(ffa_w_in), bf(ffa_w_out), bf(ffb_w_in), bf(ffb_w_out)
    w_o, w_g, w_p = bf(mix_w_o), bf(ple_w_gate), bf(ple_w_proj)
    alog, dtb = _lane_rows(gdn_a_log, A_LANE), _lane_rows(gdn_dt_bias, A_LANE)
    nw = jnp.concatenate([gdn_norm_w, gdn_norm_w], axis=-1).astype(F32)
    rope = _rope_rows()
    pos = positions.reshape(t, 1).astype(F32)
    p_rows = p.reshape(depth, t, p.shape[-1])

    h = x.reshape(t, d)
    for i in range(depth):
        h, qkv, z, sb, m1, q_m, k_m, v_m = _ffn_inproj(i, h, pos, rope, ffa_in, ffa_out, ln_g, ln_b, w_mix,
                                                       mla_q_norm_w, mla_kv_norm_w, w_uq, w_ukv)
        o_gdn = _gdn(i, qkv, z, m1, gdn_conv_w, alog, dtb, nw, batch)
        o_sb = _sb_attn(sb, batch)
        o_mla = _mla_attn(q_m, k_m, v_m, batch)
        h = _post(i, h, o_gdn, o_sb, o_mla, w_o, ln_g, ln_b, ffb_in, ffb_out, p_rows, w_g, w_p)
    return h.reshape(batch, seq, d)
```

```python
import functools

import jax
import jax.numpy as jnp
from jax import lax
from jax.experimental import pallas as pl
from jax.experimental.pallas import tpu as pltpu

F32 = jnp.float32
BF16 = jnp.bfloat16

DEPTH = 2
D_MODEL = 1024
PLE_DIM = 256
D_FF = 2816
LN_EPS = 1e-5
RMS_EPS = 1e-6
DEEPNORM_ALPHA = (2 * DEPTH) ** 0.25

GDN_HEADS = 8
GDN_DK = 64
GDN_DV = 64
GDN_CONV = 4
SB_HEADS = 4
SB_DIM = 64
MLA_HEADS = 4
MLA_NOPE = 64
MLA_ROPE = 32
MLA_V = 64
MLA_Q_RANK = 256
MLA_KV_RANK = 128
ROPE_BASE = 10000.0

LOG2E = 1.4426950408889634
LANES = 128
HALF = LANES // 2
VMEM_LIMIT = 56 * 1024 * 1024

GDN_CHUNK = 128
GDN_CHUNKS_PER_STEP = 4
GDN_GROUP = 2
SUB_TILE = 256
TOKEN_TILE = 512
POST_TILE = 512
SB_QUERY_BLOCK = 512
SB_STEP_GROUPS = 2
SB_DEAD_LOG2 = 150.0
MLA_BLOCK = 1024
MLA_KEY_BLOCK = 1024

N_HD = GDN_HEADS * GDN_DK
N_QKV = 3 * N_HD
N_Z = GDN_HEADS * GDN_DV
N_SB = 3 * SB_HEADS * SB_DIM
N_ML = MLA_Q_RANK + MLA_KV_RANK + 2 * LANES
A_LANE = 0
B_LANE = GDN_HEADS
N_MIX = N_QKV + N_Z + N_SB + N_ML
N_MLA_QK = MLA_HEADS * LANES
N_MLA_V = MLA_HEADS * MLA_V
ROPE_ROWS = 8


def _sigmoid(x):
    return 0.5 * jnp.tanh(0.5 * x) + 0.5


def _silu(x):
    half = 0.5 * x
    return half * jnp.tanh(half) + half


def _softplus(x):
    return jnp.maximum(x, 0.0) + jnp.log1p(jnp.exp(-jnp.abs(x)))


def _layer_norm(r, g, b):
    mu = jnp.mean(r, axis=-1, keepdims=True)
    d = r - mu
    var = jnp.mean(d * d, axis=-1, keepdims=True)
    return d * lax.rsqrt(var + LN_EPS) * g + b


def _rms_norm(x, w):
    return x * lax.rsqrt(jnp.mean(x * x, axis=-1, keepdims=True) + RMS_EPS) * w


def _dot(a, b):
    return jnp.dot(a, b, preferred_element_type=F32)


def _dot_nt(a, b):
    return lax.dot_general(a, b, (((1,), (1,)), ((), ())), preferred_element_type=F32)


def _head_rows(x, lo):
    zero = jnp.zeros_like(x)
    return jnp.concatenate([jnp.where(lo, x, zero), jnp.where(lo, zero, x)], axis=0)


def _replace_tail(full, r0, tail):
    return tail if r0 == 0 else jnp.concatenate([full[:r0], tail], axis=0)


def _split2(x):
    hi = x.astype(BF16)
    lo = (x - hi.astype(F32)).astype(BF16)
    return hi, lo


def _split3(x):
    hi = x.astype(BF16)
    r = x - hi.astype(F32)
    mid = r.astype(BF16)
    lo = (r - mid.astype(F32)).astype(BF16)
    return hi, mid, lo


def _swiglu(xb, w_in_ref, w_out_ref):
    gu = _dot(xb, w_in_ref[...])
    act = _silu(gu[:, :D_FF]) * gu[:, D_FF:]
    return _dot(act.astype(BF16), w_out_ref[...])


def _sub_tiles(rows):
    return [slice(s, s + SUB_TILE) for s in range(0, rows, SUB_TILE)]


def _round_robin(gens):
    live = list(gens)
    while live:
        for g in list(live):
            try:
                next(g)
            except StopIteration:
                live.remove(g)


def _whole(a):
    return pl.BlockSpec(a.shape, lambda *_: (0,) * a.ndim, pipeline_mode=pl.Buffered(1))


def _of_layer(a, layer, block=None, at=None):
    tail = tuple(a.shape[1:]) if block is None else tuple(block)
    idx = (0,) * len(tail) if at is None else tuple(at)
    return pl.BlockSpec((None,) + tail, lambda *_: (layer,) + idx, pipeline_mode=pl.Buffered(1))


def _params(n_axes):
    return pltpu.CompilerParams(dimension_semantics=("arbitrary",) * n_axes, vmem_limit_bytes=VMEM_LIMIT)


def _ffn_inproj_kernel(layer, h_ref, pos_ref, rope_ref, w_in_ref, w_out_ref, g_ref, b_ref,
                       w_gdn_ref, w_att_ref, w_m_ref, qnw_ref, kvnw_ref, w_uq_ref, w_ukv_ref,
                       h_out, qkv_out, z_out, sb_out, m1_out, mq_out, mk_out, mv_out):
    lane = lax.broadcasted_iota(jnp.int32, (1, LANES), 1)
    rot = (lane >= MLA_NOPE) & (lane < MLA_NOPE + MLA_ROPE)
    n_q = SB_HEADS * SB_DIM
    scale = (MLA_NOPE + MLA_ROPE) ** -0.5 * LOG2E

    def rows_of(r):
        x = h_ref[r, :]
        gu = _dot(x.astype(BF16), w_in_ref[...])
        yield
        y = _dot((_silu(gu[:, :D_FF]) * gu[:, D_FF:]).astype(BF16), w_out_ref[...])
        ang = pos_ref[r, :] * rope_ref[0:1, :]
        cos = jnp.cos(ang) * rope_ref[1:2, :]
        sin = jnp.sin(ang) * rope_ref[2:3, :]
        yield
        hn = _layer_norm(DEEPNORM_ALPHA * x + 0.5 * y, g_ref[0:1, :], b_ref[0:1, :])
        h_out[r, :] = hn
        hb = hn.astype(BF16)
        gdn = _dot(hb, w_gdn_ref[...])
        att = _dot(hb, w_att_ref[...])
        m12 = _dot(hb, w_m_ref[...])
        yield
        qkv_out[r, :] = gdn[:, :N_QKV]
        z_out[r, :] = gdn[:, N_QKV:]
        sb_out[r, :n_q] = (att[:, :n_q] * (SB_DIM ** -0.5 * LOG2E)).astype(BF16)
        sb_out[r, n_q:] = att[:, n_q:N_SB].astype(BF16)
        mq = att[:, N_SB:N_SB + MLA_Q_RANK]
        ckv = att[:, N_SB + MLA_Q_RANK:]
        m1 = m12[:, :LANES]
        m2 = m12[:, LANES:]
        m1_out[r, :] = m1
        qf = _dot(_rms_norm(mq, qnw_ref[layer:layer + 1, :]).astype(BF16), w_uq_ref[...])
        kv = _dot(_rms_norm(ckv, kvnw_ref[layer:layer + 1, :]).astype(BF16), w_ukv_ref[...])
        yield
        k_rot = jnp.where(rot, m1 * cos, 0.0) + m2 * sin
        for h in range(MLA_HEADS):
            sl = slice(h * LANES, (h + 1) * LANES)
            q_h = qf[:, sl] * cos + qf[:, N_MLA_QK + h * LANES:N_MLA_QK + (h + 1) * LANES] * sin
            mq_out[r, sl] = (q_h * scale).astype(BF16)
            mk_out[r, sl] = (kv[:, sl] + k_rot).astype(BF16)
        mv_out[r, :] = kv[:, N_MLA_QK:].astype(BF16)

    _round_robin([rows_of(r) for r in _sub_tiles(h_ref.shape[0])])


def _ffn_inproj(layer, h, pos, rope, w_in, w_out, ln_g, ln_b, w_mix, qnw, kvnw, w_uq, w_ukv):
    w_gdn, w_att, w_m = w_mix
    t = h.shape[0]
    tm = TOKEN_TILE
    row = lambda n: pl.BlockSpec((tm, n), lambda i: (i, 0))
    widths = (D_MODEL, N_QKV, N_Z, N_SB, LANES, N_MLA_QK, N_MLA_QK, N_MLA_V)
    dtypes = (F32, F32, F32, BF16, F32, BF16, BF16, BF16)
    return pl.pallas_call(
        functools.partial(_ffn_inproj_kernel, layer),
        grid=(t // tm,),
        in_specs=[row(D_MODEL), row(1), _whole(rope), _of_layer(w_in, layer), _of_layer(w_out, layer),
                  _of_layer(ln_g, layer), _of_layer(ln_b, layer),
                  _of_layer(w_gdn, layer), _of_layer(w_att, layer), _of_layer(w_m, layer),
                  _whole(qnw), _whole(kvnw), _of_layer(w_uq, layer), _of_layer(w_ukv, layer)],
        out_specs=[row(n) for n in widths],
        out_shape=[jax.ShapeDtypeStruct((t, n), dt) for n, dt in zip(widths, dtypes)],
        compiler_params=_params(1),
        name="ffn_inproj",
    )(h, pos, rope, w_in, w_out, ln_g, ln_b, w_gdn, w_att, w_m, qnw, kvnw, w_uq, w_ukv)


def _unit_lower_inverses(l_strict, row, col):
    c = l_strict[0].shape[0]
    blk = lambda v, log2: jnp.right_shift(v, log2)
    pair = (blk(row, 1) == blk(col, 1)) & (row > col)
    eye = jnp.where(row == col, 1.0, 0.0)
    xs = [eye - jnp.where(pair, l, 0.0) for l in l_strict]
    log2 = 1
    while (2 << log2) <= c:
        off = (blk(row, log2 + 1) == blk(col, log2 + 1)) & (blk(row, log2) > blk(col, log2))
        xb = [x.astype(BF16) for x in xs]
        lx = [_dot(jnp.where(off, l, 0.0).astype(BF16), b).astype(BF16) for l, b in zip(l_strict, xb)]
        yield
        xs = [x - _dot(b, t) for x, b, t in zip(xs, xb, lx)]
        yield
        log2 += 1
    return xs


def _interleave(main, side):
    values, live = [None, None], [main, side]
    while any(g is not None for g in live):
        for n, g in enumerate(live):
            if g is not None:
                try:
                    next(g)
                except StopIteration as stop:
                    values[n], live[n] = stop.value, None
    return values


def _dot_exact(a, b, terms):
    split = {2: _split2, 3: _split3}[terms]
    if a.dtype == F32:
        parts = [_dot(piece, b) for piece in split(a)]
    else:
        parts = [_dot(a, piece) for piece in split(b)]
    return functools.reduce(lambda x, y: x + y, parts)


def _gdn_kernel(layer, qkv_ref, z_ref, m1_ref, cw_ref, alog_ref, dtb_ref, nw_ref, o_ref, buf, state_ref):
    c = GDN_CHUNK
    chunks = qkv_ref.shape[0] // c
    pairs = GDN_HEADS // 2
    first = pl.program_id(1) == 0

    @pl.when(first)
    def _():
        state_ref[...] = jnp.zeros(state_ref.shape, F32)

    lane = lax.broadcasted_iota(jnp.int32, (1, LANES), 1)
    lo = lane < HALF
    row = lax.broadcasted_iota(jnp.int32, (c, c), 0)
    col = lax.broadcasted_iota(jnp.int32, (c, c), 1)
    incl = row >= col
    strict = row > col
    tril = jnp.where(incl, 1.0, 0.0).astype(BF16)
    same_head = jnp.where(jnp.right_shift(row, 6) == jnp.right_shift(col, 6), 1.0, 0.0).astype(BF16)

    def _head_sums(sq):
        return _dot(sq.astype(BF16), same_head)

    a_row = -jnp.exp(alog_ref[layer:layer + 1, :])
    dt_row = dtb_ref[layer:layer + 1, :]
    nw = nw_ref[layer:layer + 1, :]
    heads = range(GDN_HEADS)
    hs = [slice(h * LANES, (h + 1) * LANES) for h in heads]
    ps = [slice(p * LANES, (p + 1) * LANES) for p in range(pairs)]

    @pl.when(first)
    def _():
        buf[...] = jnp.zeros(buf.shape, F32)


    def prepare(j):
        rows = slice(j * c, (j + 1) * c)

        def conv_silu(group):
            sl = slice(group * LANES, (group + 1) * LANES)
            x = qkv_ref[rows, sl]
            before = buf[:, sl] if j == 0 else qkv_ref[j * c - 8:j * c, sl]
            xx = jnp.concatenate([before, x], axis=0)
            y = x * cw_ref[GDN_CONV - 1:GDN_CONV, sl]
            for tap in range(GDN_CONV - 1):
                y = y + pltpu.roll(xx, GDN_CONV - 1 - tap, axis=0)[8:] * cw_ref[tap:tap + 1, sl]
            return _silu(y)

        m1 = m1_ref[rows, :]
        gc = _dot_exact(tril, a_row * _softplus(m1 + dt_row), 3)
        gc_rows = gc.T
        beta_all = _sigmoid(m1)
        yield
        out = []
        for p in range(pairs):
            xq, xk, xv = conv_silu(p), conv_silu(pairs + p), conv_silu(2 * pairs + p)
            yield
            qn = xq * (lax.rsqrt(_head_sums(xq * xq) + RMS_EPS) * (GDN_DK ** -0.5))
            kn = xk * lax.rsqrt(_head_sums(xk * xk) + RMS_EPS)
            yield
            for h in (2 * p, 2 * p + 1):
                mine = lo if h % 2 == 0 else jnp.logical_not(lo)
                gcb = jnp.broadcast_to(gc[:, A_LANE + h:A_LANE + h + 1], (c, LANES))
                gc_row = gc_rows[A_LANE + h:A_LANE + h + 1, :]
                out.append(dict(
                    q=jnp.where(mine, qn, 0.0), k=jnp.where(mine, kn, 0.0), v=jnp.where(mine, xv, 0.0),
                    gcb=gcb, beta=jnp.broadcast_to(beta_all[:, B_LANE + h:B_LANE + h + 1], (c, LANES)),
                    decay=jnp.where(incl, jnp.exp(jnp.where(incl, gcb - gc_row, 0.0)), 0.0), e=jnp.exp(gcb)))
                yield
        return out

    def solve(pre):
        gram = [_dot_nt(jnp.concatenate([t["q"], t["k"]], axis=0).astype(BF16), t["k"].astype(BF16))
                for t in pre]
        yield
        l_strict = [jnp.where(strict, g[c:] * t["decay"] * t["beta"], 0.0) for g, t in zip(gram, pre)]
        t_inv = yield from _unit_lower_inverses(l_strict, row, col)
        out = []
        for g, t, ti in zip(gram, pre, t_inv):
            uw = _dot(ti.astype(BF16), jnp.concatenate([t["v"] * t["beta"], t["k"] * (t["beta"] * t["e"])],
                                                       axis=1).astype(BF16))
            g_end = t["gcb"][c - 1:c, :]
            out.append(dict(
                u=uw[:, :LANES], wq=jnp.concatenate([uw[:, LANES:], t["q"] * t["e"]], axis=0).astype(BF16),
                qk_kd=jnp.concatenate([g[:c] * t["decay"], (t["k"] * jnp.exp(g_end - t["gcb"])).T],
                                      axis=0).astype(BF16),
                keep=jnp.exp(g_end)))
        yield
        return out

    def advance(j, sol, state):
        ws = [_dot(s["wq"], st.astype(BF16)) for s, st in zip(sol, state)]
        yield
        fin = [_dot(s["qk_kd"], (s["u"] - w[:c]).astype(BF16)) for s, w in zip(sol, ws)]
        yield
        new_state = [st * s["keep"] + f[c:] for st, s, f in zip(state, sol, fin)]
        rows = slice(j * c, (j + 1) * c)
        for p in range(pairs):
            o_pair = (ws[2 * p][c:] + fin[2 * p][:c]) + (ws[2 * p + 1][c:] + fin[2 * p + 1][:c])
            ms = _head_sums(o_pair * o_pair) * (1.0 / GDN_DV)
            o_ref[rows, ps[p]] = (o_pair * lax.rsqrt(ms + RMS_EPS) * nw * _silu(z_ref[rows, ps[p]])).astype(o_ref.dtype)
            yield
        return new_state

    def prepare_group(js):
        out = []
        for j in js:
            out += yield from prepare(j)
        return out

    def group_matmuls(js, pre, state):
        sol = yield from solve(pre)
        for n, j in enumerate(js):
            state = yield from advance(j, sol[n * GDN_HEADS:(n + 1) * GDN_HEADS], state)
        return state

    def nothing():
        return None
        yield

    groups = [list(range(g0, min(g0 + GDN_GROUP, chunks))) for g0 in range(0, chunks, GDN_GROUP)]
    state = [state_ref[h] for h in heads]
    _, pre = _interleave(nothing(), prepare_group(groups[0]))
    for n, js in enumerate(groups):
        upcoming = prepare_group(groups[n + 1]) if n + 1 < len(groups) else nothing()
        state, pre = _interleave(group_matmuls(js, pre, state), upcoming)
    for h in heads:
        state_ref[h] = state[h]
    buf[...] = qkv_ref[chunks * c - 8:chunks * c, :]


def _gdn(layer, qkv, z, m1, conv_w, alog, dtb, nw, batch):
    t = qkv.shape[0]
    c = GDN_CHUNK * GDN_CHUNKS_PER_STEP
    nc = t // batch // c
    blk = lambda n: pl.BlockSpec((c, n), lambda b, i: (b * nc + i, 0))
    return pl.pallas_call(
        functools.partial(_gdn_kernel, layer),
        grid=(batch, nc),
        in_specs=[blk(N_QKV), blk(N_Z), blk(LANES), _of_layer(conv_w, layer),
                  _whole(alog), _whole(dtb), _whole(nw)],
        out_specs=blk(GDN_HEADS * GDN_DV),
        out_shape=jax.ShapeDtypeStruct((t, GDN_HEADS * GDN_DV), BF16),
        scratch_shapes=[pltpu.VMEM((8, N_QKV), F32), pltpu.VMEM((GDN_HEADS, LANES, LANES), F32)],
        compiler_params=_params(2),
        name="gdn",
    )(qkv, z, m1, conv_w, alog, dtb, nw)


def _sb_kernel(q_ref, k_ref, v_ref, o_ref):
    bq = q_ref.shape[0]
    g = LANES
    per = bq // g
    i = pl.program_id(2)
    lane = lax.broadcasted_iota(jnp.int32, (1, LANES), 1)
    lo = lane < HALF
    q = q_ref[...]
    row = lax.broadcasted_iota(jnp.int32, (bq, g), 0)
    col = lax.broadcasted_iota(jnp.int32, (bq, g), 1)
    jr = lax.broadcasted_iota(jnp.int32, (g, g), 0)
    jc = lax.broadcasted_iota(jnp.int32, (g, g), 1)
    later = jnp.where(jr > jc, 1.0, 0.0).astype(BF16)
    suffix_total = jnp.concatenate([later, jnp.ones((g, g), BF16)], axis=1)
    suffix_total = jnp.concatenate([suffix_total, suffix_total], axis=0)

    def group(start, run, acc, mask, r0):
        k_g = k_ref[pl.ds(start, g), :]
        v_g = v_ref[pl.ds(start, g), :]
        z = _dot_nt(q[r0:], _head_rows(k_g, lo))
        nl = jnp.maximum(z, 0.0) + jnp.log(1.0 + jnp.exp2(-jnp.abs(z))) * LOG2E
        if mask is not None:
            nl = jnp.where(mask, nl, 0.0)
        wts, new_run = [], []
        for hh in range(2):
            sl = slice(hh * g, (hh + 1) * g)
            hi, low = _split2(nl[:, sl])
            st = _dot(jnp.concatenate([hi, low], axis=1), suffix_total)
            wts.append(jnp.exp2(z[:, sl] - nl[:, sl] - (run[hh][r0:] + st[:, :g])))
            new_run.append(_replace_tail(run[hh], r0, run[hh][r0:] + st[:, g:]))
        wts = jnp.concatenate(wts, axis=1)
        if mask is not None:
            wts = jnp.where(mask, wts, 0.0)
        pv = _dot(wts.astype(BF16), _head_rows(v_g, lo))
        return tuple(new_run), _replace_tail(acc, r0, acc[r0:] + pv)

    run = (jnp.zeros((bq, g), F32), jnp.zeros((bq, g), F32))
    acc = jnp.zeros((bq, LANES), F32)
    base = i * bq
    for d in range(per):
        off = (per - 1 - d) * g
        mask = ((col + off) < row)[off:]
        run, acc = group(pl.multiple_of(base + off, g), run, acc, jnp.concatenate([mask, mask], axis=1), off)

    step = SB_STEP_GROUPS * g
    steps = i * (bq // step)

    def lowest(run):
        return jnp.min(jnp.minimum(run[0], run[1]))

    def more(carry):
        n, low, _, _ = carry
        return jnp.logical_and(n < steps, low < SB_DEAD_LOG2)

    def walk(carry):
        n, _, run, acc = carry
        start = base - (n + 1) * step
        for d in range(SB_STEP_GROUPS):
            run, acc = group(pl.multiple_of(start + (SB_STEP_GROUPS - 1 - d) * g, g), run, acc, None, 0)
        return n + 1, lowest(run), run, acc

    _, _, run, acc = lax.while_loop(more, walk, (jnp.int32(0), lowest(run), run, acc))
    o_ref[...] = acc.astype(o_ref.dtype)


def _sb_attn(sb, batch):
    t = sb.shape[0]
    s = t // batch
    bq = SB_QUERY_BLOCK
    nq = s // bq
    pairs = SB_HEADS // 2
    return pl.pallas_call(
        _sb_kernel,
        grid=(batch, pairs, nq),
        in_specs=[pl.BlockSpec((bq, LANES), lambda b, p, i: (b * nq + i, p)),
                  pl.BlockSpec((s, LANES), lambda b, p, i: (b, pairs + p)),
                  pl.BlockSpec((s, LANES), lambda b, p, i: (b, 2 * pairs + p))],
        out_specs=pl.BlockSpec((bq, LANES), lambda b, p, i: (b * nq + i, p)),
        out_shape=jax.ShapeDtypeStruct((t, SB_HEADS * SB_DIM), BF16),
        compiler_params=_params(3),
        name="sb_attn",
    )(sb, sb, sb)


def _mla_kernel(q_ref, k_ref, v_ref, o_ref):
    bq = q_ref.shape[0]
    bk = MLA_KEY_BLOCK
    per = bq // bk
    i = pl.program_id(2)
    lane = lax.broadcasted_iota(jnp.int32, (1, LANES), 1)
    lo = lane < HALF
    q = q_ref[...]
    row = lax.broadcasted_iota(jnp.int32, (bq, bk), 0)
    col = lax.broadcasted_iota(jnp.int32, (bq, bk), 1)

    def visit(start, carry, mask, r0):
        stats, acc = carry
        k_b = k_ref[pl.ds(start, bk), :]
        v_b = v_ref[pl.ds(start, bk), :]
        new, probs, alphas = [], [], []
        for hh in range(2):
            m, l = stats[hh]
            sl = slice(hh * LANES, (hh + 1) * LANES)
            s = _dot_nt(q[r0:, sl], k_b[:, sl])
            if mask is not None:
                s = jnp.where(mask, s, -jnp.inf)
            m_new = jnp.maximum(m[r0:], jnp.max(s, axis=-1, keepdims=True))
            alpha = jnp.exp2(m[r0:] - m_new)
            p = jnp.exp2(s - m_new)
            new.append((_replace_tail(m, r0, m_new),
                        _replace_tail(l, r0, alpha * l[r0:] + jnp.sum(p, axis=-1, keepdims=True))))
            probs.append(p.astype(BF16))
            alphas.append(alpha)
        pv = _dot(jnp.concatenate(probs, axis=1), _head_rows(v_b, lo))
        return tuple(new), _replace_tail(acc, r0, jnp.where(lo, alphas[0], alphas[1]) * acc[r0:] + pv)

    carry = (tuple((jnp.full((bq, 1), -jnp.inf, F32), jnp.zeros((bq, 1), F32)) for _ in range(2)),
             jnp.zeros((bq, LANES), F32))
    base = i * bq
    for d in range(per):
        mask = ((col + d * bk) <= row)[d * bk:]
        carry = visit(pl.multiple_of(base + d * bk, bk), carry, mask, d * bk)

    def block(n, cr):
        for d in range(per):
            cr = visit(pl.multiple_of(n * bq + d * bk, bk), cr, None, 0)
        return cr

    stats, acc = lax.fori_loop(0, i, block, carry)
    o_ref[...] = (acc / jnp.where(lo, stats[0][1], stats[1][1])).astype(o_ref.dtype)


def _mla_attn(q, k, v, batch):
    t = q.shape[0]
    s = t // batch
    bq = MLA_BLOCK
    nq = s // bq
    pairs = MLA_HEADS // 2
    return pl.pallas_call(
        _mla_kernel,
        grid=(batch, pairs, nq),
        in_specs=[pl.BlockSpec((bq, 2 * LANES), lambda b, p, i: (b * nq + i, p)),
                  pl.BlockSpec((s, 2 * LANES), lambda b, p, i: (b, p)),
                  pl.BlockSpec((s, LANES), lambda b, p, i: (b, p))],
        out_specs=pl.BlockSpec((bq, LANES), lambda b, p, i: (b * nq + i, p)),
        out_shape=jax.ShapeDtypeStruct((t, MLA_HEADS * MLA_V), BF16),
        compiler_params=_params(3),
        name="mla_attn",
    )(q, k, v)


def _post_kernel(h_ref, og_ref, os_ref, om_ref, w_o_ref, g_ref, b_ref, w_in_ref, w_out_ref,
                 p_ref, w_g_ref, w_p_ref, out_ref):
    n_g = GDN_HEADS * GDN_DV
    n_s = SB_HEADS * SB_DIM

    def rows_of(r):
        x = h_ref[r, :]
        mix = (_dot(og_ref[r, :], w_o_ref[:n_g, :]) + _dot(os_ref[r, :], w_o_ref[n_g:n_g + n_s, :])
               + _dot(om_ref[r, :], w_o_ref[n_g + n_s:, :]))
        yield
        h1 = _layer_norm(DEEPNORM_ALPHA * x + mix, g_ref[1:2, :], b_ref[1:2, :])
        gu = _dot(h1.astype(BF16), w_in_ref[...])
        yield
        y = _dot((_silu(gu[:, :D_FF]) * gu[:, D_FF:]).astype(BF16), w_out_ref[...])
        yield
        h2 = _layer_norm(DEEPNORM_ALPHA * h1 + 0.5 * y, g_ref[2:3, :], b_ref[2:3, :])
        gate = _dot(h2.astype(BF16), w_g_ref[...])
        emb = _dot(p_ref[r, :].astype(BF16), w_p_ref[...])
        yield
        out_ref[r, :] = h2 + _sigmoid(gate) * emb

    _round_robin([rows_of(r) for r in _sub_tiles(h_ref.shape[0])])


def _post(layer, h, o_gdn, o_sb, o_mla, w_o, ln_g, ln_b, w_in, w_out, p, w_g, w_p):
    t = h.shape[0]
    tm = POST_TILE
    row = lambda n: pl.BlockSpec((tm, n), lambda i: (i, 0))
    return pl.pallas_call(
        _post_kernel,
        grid=(t // tm,),
        in_specs=[row(D_MODEL), row(o_gdn.shape[1]), row(o_sb.shape[1]), row(o_mla.shape[1]),
                  _of_layer(w_o, layer), _of_layer(ln_g, layer), _of_layer(ln_b, layer),
                  _of_layer(w_in, layer), _of_layer(w_out, layer),
                  pl.BlockSpec((None, tm, PLE_DIM), lambda i: (layer, i, 0)),
                  _of_layer(w_g, layer), _of_layer(w_p, layer)],
        out_specs=row(D_MODEL),
        out_shape=jax.ShapeDtypeStruct((t, D_MODEL), F32),
        compiler_params=_params(1),
        name="post",
    )(h, o_gdn, o_sb, o_mla, w_o, ln_g, ln_b, w_in, w_out, p, w_g, w_p)


def _zeros_like_cols(w, n):
    return jnp.zeros(w.shape[:-1] + (n,), w.dtype)


def _regroup_mix(w):
    edges = [N_QKV + N_Z, GDN_HEADS, GDN_HEADS, N_SB + MLA_Q_RANK + MLA_KV_RANK, MLA_ROPE]
    parts, o = [], 0
    for n in edges:
        parts.append(w[..., o:o + n])
        o += n
    gdn, ga, gb, att, kr = parts
    half = MLA_ROPE // 2
    pad = LANES - MLA_NOPE - MLA_ROPE
    m1 = [ga, gb, _zeros_like_cols(w, MLA_NOPE - 2 * GDN_HEADS), kr, _zeros_like_cols(w, pad)]
    m2 = [_zeros_like_cols(w, MLA_NOPE), kr[..., half:], kr[..., :half], _zeros_like_cols(w, pad)]
    return gdn.astype(BF16), att.astype(BF16), jnp.concatenate(m1 + m2, axis=-1).astype(BF16)


def _regroup_uq(w):
    d = MLA_NOPE + MLA_ROPE
    half = MLA_ROPE // 2
    pad = _zeros_like_cols(w, LANES - d)
    main, swap = [], []
    for h in range(MLA_HEADS):
        main += [w[..., h * d:(h + 1) * d], pad]
        swap += [_zeros_like_cols(w, MLA_NOPE), w[..., h * d + MLA_NOPE + half:(h + 1) * d],
                 w[..., h * d + MLA_NOPE:h * d + MLA_NOPE + half], pad]
    return jnp.concatenate(main + swap, axis=-1)


def _regroup_ukv(w):
    d = MLA_NOPE + MLA_V
    keys, vals = [], []
    for h in range(MLA_HEADS):
        keys += [w[..., h * d:h * d + MLA_NOPE], _zeros_like_cols(w, LANES - MLA_NOPE)]
        vals.append(w[..., h * d + MLA_NOPE:(h + 1) * d])
    return jnp.concatenate(keys + vals, axis=-1)


def _lane_rows(vals, start):
    return jnp.pad(vals.astype(F32), ((0, 0), (start, LANES - start - vals.shape[1])))


def _rope_rows():
    inv = 1.0 / (ROPE_BASE ** (jnp.arange(0, MLA_ROPE, 2, dtype=F32) / MLA_ROPE))
    half = MLA_ROPE // 2
    pad = LANES - MLA_NOPE - MLA_ROPE
    z = lambda n: jnp.zeros((n,), F32)
    one = lambda n: jnp.ones((n,), F32)
    freq = jnp.concatenate([z(MLA_NOPE), inv, inv, z(pad)])
    cos_mask = jnp.concatenate([one(MLA_NOPE + MLA_ROPE), z(pad)])
    sin_sign = jnp.concatenate([z(MLA_NOPE), -one(half), one(half), z(pad)])
    rows = jnp.stack([freq, cos_mask, sin_sign])
    return jnp.pad(rows, ((0, ROPE_ROWS - rows.shape[0]), (0, 0)))


def kernel(x, p, positions, ffa_w_in, ffa_w_out, mix_w_in, gdn_conv_w, gdn_a_log, gdn_dt_bias, gdn_norm_w, mla_q_norm_w, mla_kv_norm_w, mla_w_uq, mla_w_ukv, mix_w_o, ffb_w_in, ffb_w_out, ln_g, ln_b, ple_w_gate, ple_w_proj):
    batch, seq, d = x.shape
    t = batch * seq
    depth = p.shape[0]
    hd = GDN_HEADS * GDN_DK

    bf = lambda w: w.astype(BF16)
    w_mix, w_uq, w_ukv = _regroup_mix(mix_w_in), bf(_regroup_uq(mla_w_uq)), bf(_regroup_ukv(mla_w_ukv))
    ffa_in, ffa_out, ffb_in, ffb_out = bf(ffa_w_in), bf(ffa_w_out), bf(ffb_w_in), bf(ffb_w_out)
    w_o, w_g, w_p = bf(mix_w_o), bf(ple_w_gate), bf(ple_w_proj)
    alog, dtb = _lane_rows(gdn_a_log, A_LANE), _lane_rows(gdn_dt_bias, A_LANE)
    nw = jnp.concatenate([gdn_norm_w, gdn_norm_w], axis=-1).astype(F32)
    rope = _rope_rows()
    pos = positions.reshape(t, 1).astype(F32)
    p_rows = p.reshape(depth, t, p.shape[-1])

    h = x.reshape(t, d)
    for i in range(depth):
        h, qkv, z, sb, m1, q_m, k_m, v_m = _ffn_inproj(i, h, pos, rope, ffa_in, ffa_out, ln_g, ln_b, w_mix,
                                                       mla_q_norm_w, mla_kv_norm_w, w_uq, w_ukv)
        o_gdn = _gdn(i, qkv, z, m1, gdn_conv_w, alog, dtb, nw, batch)
        o_sb = _sb_attn(sb, batch)
        o_mla = _mla_attn(q_m, k_m, v_m, batch)
        h = _post(i, h, o_gdn, o_sb, o_mla, w_o, ln_g, ln_b, ffb_in, ffb_out, p_rows, w_g, w_p)
    return h.reshape(batch, seq, d)
```

```python
import functools

import jax
import jax.numpy as jnp
from jax import lax
from jax.experimental import pallas as pl
from jax.experimental.pallas import tpu as pltpu

F32 = jnp.float32
BF16 = jnp.bfloat16

DEPTH = 2
D_MODEL = 1024
PLE_DIM = 256
D_FF = 2816
LN_EPS = 1e-5
RMS_EPS = 1e-6
DEEPNORM_ALPHA = (2 * DEPTH) ** 0.25

GDN_HEADS = 8
GDN_DK = 64
GDN_DV = 64
GDN_CONV = 4
SB_HEADS = 4
SB_DIM = 64
MLA_HEADS = 4
MLA_NOPE = 64
MLA_ROPE = 32
MLA_V = 64
MLA_Q_RANK = 256
MLA_KV_RANK = 128
ROPE_BASE = 10000.0

LOG2E = 1.4426950408889634
LANES = 128
HALF = LANES // 2
VMEM_LIMIT = 56 * 1024 * 1024

GDN_CHUNK = 128
GDN_CHUNKS_PER_STEP = 4
GDN_GROUP = 2
SUB_TILE = 256
TOKEN_TILE = 512
POST_TILE = 512
SB_QUERY_BLOCK = 512
SB_STEP_GROUPS = 2
SB_DEAD_LOG2 = 150.0
MLA_BLOCK = 1024
MLA_KEY_BLOCK = 1024

N_HD = GDN_HEADS * GDN_DK
N_QKV = 3 * N_HD
N_Z = GDN_HEADS * GDN_DV
N_SB = 3 * SB_HEADS * SB_DIM
N_ML = MLA_Q_RANK + MLA_KV_RANK + 2 * LANES
A_LANE = 0
B_LANE = GDN_HEADS
N_MIX = N_QKV + N_Z + N_SB + N_ML
N_MLA_QK = MLA_HEADS * LANES
N_MLA_V = MLA_HEADS * MLA_V
ROPE_ROWS = 8


def _sigmoid(x):
    return 0.5 * jnp.tanh(0.5 * x) + 0.5


def _silu(x):
    half = 0.5 * x
    return half * jnp.tanh(half) + half


def _softplus(x):
    return jnp.maximum(x, 0.0) + jnp.log1p(jnp.exp(-jnp.abs(x)))


def _layer_norm(r, g, b):
    mu = jnp.mean(r, axis=-1, keepdims=True)
    d = r - mu
    var = jnp.mean(d * d, axis=-1, keepdims=True)
    return d * lax.rsqrt(var + LN_EPS) * g + b


def _rms_norm(x, w):
    return x * lax.rsqrt(jnp.mean(x * x, axis=-1, keepdims=True) + RMS_EPS) * w


def _dot(a, b):
    return jnp.dot(a, b, preferred_element_type=F32)


def _dot_nt(a, b):
    return lax.dot_general(a, b, (((1,), (1,)), ((), ())), preferred_element_type=F32)


def _head_rows(x, lo):
    zero = jnp.zeros_like(x)
    return jnp.concatenate([jnp.where(lo, x, zero), jnp.where(lo, zero, x)], axis=0)


def _replace_tail(full, r0, tail):
    return tail if r0 == 0 else jnp.concatenate([full[:r0], tail], axis=0)


def _split2(x):
    hi = x.astype(BF16)
    lo = (x - hi.astype(F32)).astype(BF16)
    return hi, lo


def _split3(x):
    hi = x.astype(BF16)
    r = x - hi.astype(F32)
    mid = r.astype(BF16)
    lo = (r - mid.astype(F32)).astype(BF16)
    return hi, mid, lo


def _swiglu(xb, w_in_ref, w_out_ref):
    gu = _dot(xb, w_in_ref[...])
    act = _silu(gu[:, :D_FF]) * gu[:, D_FF:]
    return _dot(act.astype(BF16), w_out_ref[...])


def _sub_tiles(rows):
    return [slice(s, s + SUB_TILE) for s in range(0, rows, SUB_TILE)]


def _round_robin(gens):
    live = list(gens)
    while live:
        for g in list(live):
            try:
                next(g)
            except StopIteration:
                live.remove(g)


def _whole(a):
    return pl.BlockSpec(a.shape, lambda *_: (0,) * a.ndim, pipeline_mode=pl.Buffered(1))


def _of_layer(a, layer, block=None, at=None):
    tail = tuple(a.shape[1:]) if block is None else tuple(block)
    idx = (0,) * len(tail) if at is None else tuple(at)
    return pl.BlockSpec((None,) + tail, lambda *_: (layer,) + idx, pipeline_mode=pl.Buffered(1))


def _params(n_axes):
    return pltpu.CompilerParams(dimension_semantics=("arbitrary",) * n_axes, vmem_limit_bytes=VMEM_LIMIT)


def _ffn_inproj_kernel(layer, h_ref, pos_ref, rope_ref, w_in_ref, w_out_ref, g_ref, b_ref,
                       w_gdn_ref, w_att_ref, w_m_ref, qnw_ref, kvnw_ref, w_uq_ref, w_ukv_ref,
                       h_out, qkv_out, z_out, sb_out, m1_out, mq_out, mk_out, mv_out):
    lane = lax.broadcasted_iota(jnp.int32, (1, LANES), 1)
    rot = (lane >= MLA_NOPE) & (lane < MLA_NOPE + MLA_ROPE)
    n_q = SB_HEADS * SB_DIM
    scale = (MLA_NOPE + MLA_ROPE) ** -0.5 * LOG2E

    def rows_of(r):
        x = h_ref[r, :]
        gu = _dot(x.astype(BF16), w_in_ref[...])
        yield
        y = _dot((_silu(gu[:, :D_FF]) * gu[:, D_FF:]).astype(BF16), w_out_ref[...])
        ang = pos_ref[r, :] * rope_ref[0:1, :]
        cos = jnp.cos(ang) * rope_ref[1:2, :]
        sin = jnp.sin(ang) * rope_ref[2:3, :]
        yield
        hn = _layer_norm(DEEPNORM_ALPHA * x + 0.5 * y, g_ref[0:1, :], b_ref[0:1, :])
        h_out[r, :] = hn
        hb = hn.astype(BF16)
        gdn = _dot(hb, w_gdn_ref[...])
        att = _dot(hb, w_att_ref[...])
        m12 = _dot(hb, w_m_ref[...])
        yield
        qkv_out[r, :] = gdn[:, :N_QKV]
        z_out[r, :] = gdn[:, N_QKV:]
        sb_out[r, :n_q] = (att[:, :n_q] * (SB_DIM ** -0.5 * LOG2E)).astype(BF16)
        sb_out[r, n_q:] = att[:, n_q:N_SB].astype(BF16)
        mq = att[:, N_SB:N_SB + MLA_Q_RANK]
        ckv = att[:, N_SB + MLA_Q_RANK:]
        m1 = m12[:, :LANES]
        m2 = m12[:, LANES:]
        m1_out[r, :] = m1
        qf = _dot(_rms_norm(mq, qnw_ref[layer:layer + 1, :]).astype(BF16), w_uq_ref[...])
        kv = _dot(_rms_norm(ckv, kvnw_ref[layer:layer + 1, :]).astype(BF16), w_ukv_ref[...])
        yield
        k_rot = jnp.where(rot, m1 * cos, 0.0) + m2 * sin
        for h in range(MLA_HEADS):
            sl = slice(h * LANES, (h + 1) * LANES)
            q_h = qf[:, sl] * cos + qf[:, N_MLA_QK + h * LANES:N_MLA_QK + (h + 1) * LANES] * sin
            mq_out[r, sl] = (q_h * scale).astype(BF16)
            mk_out[r, sl] = (kv[:, sl] + k_rot).astype(BF16)
        mv_out[r, :] = kv[:, N_MLA_QK:].astype(BF16)

    _round_robin([rows_of(r) for r in _sub_tiles(h_ref.shape[0])])


def _ffn_inproj(layer, h, pos, rope, w_in, w_out, ln_g, ln_b, w_mix, qnw, kvnw, w_uq, w_ukv):
    w_gdn, w_att, w_m = w_mix
    t = h.shape[0]
    tm = TOKEN_TILE
    row = lambda n: pl.BlockSpec((tm, n), lambda i: (i, 0))
    widths = (D_MODEL, N_QKV, N_Z, N_SB, LANES, N_MLA_QK, N_MLA_QK, N_MLA_V)
    dtypes = (F32, F32, F32, BF16, F32, BF16, BF16, BF16)
    return pl.pallas_call(
        functools.partial(_ffn_inproj_kernel, layer),
        grid=(t // tm,),
        in_specs=[row(D_MODEL), row(1), _whole(rope), _of_layer(w_in, layer), _of_layer(w_out, layer),
                  _of_layer(ln_g, layer), _of_layer(ln_b, layer),
                  _of_layer(w_gdn, layer), _of_layer(w_att, layer), _of_layer(w_m, layer),
                  _whole(qnw), _whole(kvnw), _of_layer(w_uq, layer), _of_layer(w_ukv, layer)],
        out_specs=[row(n) for n in widths],
        out_shape=[jax.ShapeDtypeStruct((t, n), dt) for n, dt in zip(widths, dtypes)],
        compiler_params=_params(1),
        name="ffn_inproj",
    )(h, pos, rope, w_in, w_out, ln_g, ln_b, w_gdn, w_att, w_m, qnw, kvnw, w_uq, w_ukv)


def _unit_lower_inverses(l_strict, row, col):
    c = l_strict[0].shape[0]
    blk = lambda v, log2: jnp.right_shift(v, log2)
    pair = (blk(row, 1) == blk(col, 1)) & (row > col)
    eye = jnp.where(row == col, 1.0, 0.0)
    xs = [eye - jnp.where(pair, l, 0.0) for l in l_strict]
    log2 = 1
    while (2 << log2) <= c:
        off = (blk(row, log2 + 1) == blk(col, log2 + 1)) & (blk(row, log2) > blk(col, log2))
        xb = [x.astype(BF16) for x in xs]
        lx = [_dot(jnp.where(off, l, 0.0).astype(BF16), b).astype(BF16) for l, b in zip(l_strict, xb)]
        yield
        xs = [x - _dot(b, t) for x, b, t in zip(xs, xb, lx)]
        yield
        log2 += 1
    return xs


def _interleave(main, side):
    values, live = [None, None], [main, side]
    while any(g is not None for g in live):
        for n, g in enumerate(live):
            if g is not None:
                try:
                    next(g)
                except StopIteration as stop:
                    values[n], live[n] = stop.value, None
    return values


def _dot_exact(a, b, terms):
    split = {2: _split2, 3: _split3}[terms]
    if a.dtype == F32:
        parts = [_dot(piece, b) for piece in split(a)]
    else:
        parts = [_dot(a, piece) for piece in split(b)]
    return functools.reduce(lambda x, y: x + y, parts)


def _gdn_kernel(layer, qkv_ref, z_ref, m1_ref, cw_ref, alog_ref, dtb_ref, nw_ref, o_ref, buf, state_ref):
    c = GDN_CHUNK
    chunks = qkv_ref.shape[0] // c
    pairs = GDN_HEADS // 2
    first = pl.program_id(1) == 0

    @pl.when(first)
    def _():
        state_ref[...] = jnp.zeros(state_ref.shape, F32)

    lane = lax.broadcasted_iota(jnp.int32, (1, LANES), 1)
    lo = lane < HALF
    row = lax.broadcasted_iota(jnp.int32, (c, c), 0)
    col = lax.broadcasted_iota(jnp.int32, (c, c), 1)
    incl = row >= col
    strict = row > col
    tril = jnp.where(incl, 1.0, 0.0).astype(BF16)
    same_head = jnp.where(jnp.right_shift(row, 6) == jnp.right_shift(col, 6), 1.0, 0.0).astype(BF16)

    def _head_sums(sq):
        return _dot(sq.astype(BF16), same_head)

    a_row = -jnp.exp(alog_ref[layer:layer + 1, :])
    dt_row = dtb_ref[layer:layer + 1, :]
    nw = nw_ref[layer:layer + 1, :]
    heads = range(GDN_HEADS)
    hs = [slice(h * LANES, (h + 1) * LANES) for h in heads]
    ps = [slice(p * LANES, (p + 1) * LANES) for p in range(pairs)]

    @pl.when(first)
    def _():
        buf[...] = jnp.zeros(buf.shape, F32)


    def prepare(j):
        rows = slice(j * c, (j + 1) * c)

        def conv_silu(group):
            sl = slice(group * LANES, (group + 1) * LANES)
            x = qkv_ref[rows, sl]
            before = buf[:, sl] if j == 0 else qkv_ref[j * c - 8:j * c, sl]
            xx = jnp.concatenate([before, x], axis=0)
            y = x * cw_ref[GDN_CONV - 1:GDN_CONV, sl]
            for tap in range(GDN_CONV - 1):
                y = y + pltpu.roll(xx, GDN_CONV - 1 - tap, axis=0)[8:] * cw_ref[tap:tap + 1, sl]
            return _silu(y)

        m1 = m1_ref[rows, :]
        gc = _dot_exact(tril, a_row * _softplus(m1 + dt_row), 3)
        gc_rows = gc.T
        beta_all = _sigmoid(m1)
        yield
        out = []
        for p in range(pairs):
            xq, xk, xv = conv_silu(p), conv_silu(pairs + p), conv_silu(2 * pairs + p)
            yield
            qn = xq * (lax.rsqrt(_head_sums(xq * xq) + RMS_EPS) * (GDN_DK ** -0.5))
            kn = xk * lax.rsqrt(_head_sums(xk * xk) + RMS_EPS)
            yield
            for h in (2 * p, 2 * p + 1):
                mine = lo if h % 2 == 0 else jnp.logical_not(lo)
                gcb = jnp.broadcast_to(gc[:, A_LANE + h:A_LANE + h + 1], (c, LANES))
                gc_row = gc_rows[A_LANE + h:A_LANE + h + 1, :]
                out.append(dict(
                    q=jnp.where(mine, qn, 0.0), k=jnp.where(mine, kn, 0.0), v=jnp.where(mine, xv, 0.0),
                    gcb=gcb, beta=jnp.broadcast_to(beta_all[:, B_LANE + h:B_LANE + h + 1], (c, LANES)),
                    decay=jnp.where(incl, jnp.exp(jnp.where(incl, gcb - gc_row, 0.0)), 0.0), e=jnp.exp(gcb)))
                yield
        return out

    def solve(pre):
        gram = [_dot_nt(jnp.concatenate([t["q"], t["k"]], axis=0).astype(BF16), t["k"].astype(BF16))
                for t in pre]
        yield
        l_strict = [jnp.where(strict, g[c:] * t["decay"] * t["beta"], 0.0) for g, t in zip(gram, pre)]
        t_inv = yield from _unit_lower_inverses(l_strict, row, col)
        out = []
        for g, t, ti in zip(gram, pre, t_inv):
            uw = _dot(ti.astype(BF16), jnp.concatenate([t["v"] * t["beta"], t["k"] * (t["beta"] * t["e"])],
                                                       axis=1).astype(BF16))
            g_end = t["gcb"][c - 1:c, :]
            out.append(dict(
                u=uw[:, :LANES], wq=jnp.concatenate([uw[:, LANES:], t["q"] * t["e"]], axis=0).astype(BF16),
                qk_kd=jnp.concatenate([g[:c] * t["decay"], (t["k"] * jnp.exp(g_end - t["gcb"])).T],
                                      axis=0).astype(BF16),
                keep=jnp.exp(g_end)))
        yield
        return out

    def advance(j, sol, state):
        ws = [_dot(s["wq"], st.astype(BF16)) for s, st in zip(sol, state)]
        yield
        fin = [_dot(s["qk_kd"], (s["u"] - w[:c]).astype(BF16)) for s, w in zip(sol, ws)]
        yield
        new_state = [st * s["keep"] + f[c:] for st, s, f in zip(state, sol, fin)]
        rows = slice(j * c, (j + 1) * c)
        for p in range(pairs):
            o_pair = (ws[2 * p][c:] + fin[2 * p][:c]) + (ws[2 * p + 1][c:] + fin[2 * p + 1][:c])
            ms = _head_sums(o_pair * o_pair) * (1.0 / GDN_DV)
            o_ref[rows, ps[p]] = (o_pair * lax.rsqrt(ms + RMS_EPS) * nw * _silu(z_ref[rows, ps[p]])).astype(o_ref.dtype)
            yield
        return new_state

    def prepare_group(js):
        out = []
        for j in js:
            out += yield from prepare(j)
        return out

    def group_matmuls(js, pre, state):
        sol = yield from solve(pre)
        for n, j in enumerate(js):
            state = yield from advance(j, sol[n * GDN_HEADS:(n + 1) * GDN_HEADS], state)
        return state

    def nothing():
        return None
        yield

    groups = [list(range(g0, min(g0 + GDN_GROUP, chunks))) for g0 in range(0, chunks, GDN_GROUP)]
    state = [state_ref[h] for h in heads]
    _, pre = _interleave(nothing(), prepare_group(groups[0]))
    for n, js in enumerate(groups):
        upcoming = prepare_group(groups[n + 1]) if n + 1 < len(groups) else nothing()
        state, pre = _interleave(group_matmuls(js, pre, state), upcoming)
    for h in heads:
        state_ref[h] = state[h]
    buf[...] = qkv_ref[chunks * c - 8:chunks * c, :]


def _gdn(layer, qkv, z, m1, conv_w, alog, dtb, nw, batch):
    t = qkv.shape[0]
    c = GDN_CHUNK * GDN_CHUNKS_PER_STEP
    nc = t // batch // c
    blk = lambda n: pl.BlockSpec((c, n), lambda b, i: (b * nc + i, 0))
    return pl.pallas_call(
        functools.partial(_gdn_kernel, layer),
        grid=(batch, nc),
        in_specs=[blk(N_QKV), blk(N_Z), blk(LANES), _of_layer(conv_w, layer),
                  _whole(alog), _whole(dtb), _whole(nw)],
        out_specs=blk(GDN_HEADS * GDN_DV),
        out_shape=jax.ShapeDtypeStruct((t, GDN_HEADS * GDN_DV), BF16),
        scratch_shapes=[pltpu.VMEM((8, N_QKV), F32), pltpu.VMEM((GDN_HEADS, LANES, LANES), F32)],
        compiler_params=_params(2),
        name="gdn",
    )(qkv, z, m1, conv_w, alog, dtb, nw)


def _sb_kernel(q_ref, k_ref, v_ref, o_ref):
    bq = q_ref.shape[0]
    g = LANES
    per = bq // g
    pairs = range(q_ref.shape[1] // LANES)
    pl_ = [slice(p * LANES, (p + 1) * LANES) for p in pairs]
    i = pl.program_id(1)
    lane = lax.broadcasted_iota(jnp.int32, (1, LANES), 1)
    lo = lane < HALF
    q = q_ref[...]
    row = lax.broadcasted_iota(jnp.int32, (bq, g), 0)
    col = lax.broadcasted_iota(jnp.int32, (bq, g), 1)
    jr = lax.broadcasted_iota(jnp.int32, (g, g), 0)
    jc = lax.broadcasted_iota(jnp.int32, (g, g), 1)
    later = jnp.where(jr > jc, 1.0, 0.0).astype(BF16)
    suffix_total = jnp.concatenate([later, jnp.ones((g, g), BF16)], axis=1)
    suffix_total = jnp.concatenate([suffix_total, suffix_total], axis=0)

    def group(start, run, acc, mask, r0):
        k_g = k_ref[pl.ds(start, g), :]
        v_g = v_ref[pl.ds(start, g), :]
        halves = [slice(hh * g, (hh + 1) * g) for hh in range(2)]
        z = [_dot_nt(q[r0:, pl_[p]], _head_rows(k_g[:, pl_[p]], lo)) for p in pairs]
        nl = [jnp.maximum(zp, 0.0) + jnp.log(1.0 + jnp.exp2(-jnp.abs(zp))) * LOG2E for zp in z]
        if mask is not None:
            nl = [jnp.where(mask, n, 0.0) for n in nl]
        st = [[_dot(jnp.concatenate(_split2(nl[p][:, sl]), axis=1), suffix_total) for sl in halves]
              for p in pairs]
        wts = [jnp.concatenate([jnp.exp2(z[p][:, sl] - nl[p][:, sl] - (run[p][hh][r0:] + st[p][hh][:, :g]))
                                for hh, sl in enumerate(halves)], axis=1) for p in pairs]
        if mask is not None:
            wts = [jnp.where(mask, w, 0.0) for w in wts]
        new_run = [tuple(_replace_tail(run[p][hh], r0, run[p][hh][r0:] + st[p][hh][:, g:]) for hh in range(2))
                   for p in pairs]
        pv = [_dot(wts[p].astype(BF16), _head_rows(v_g[:, pl_[p]], lo)) for p in pairs]
        return new_run, [_replace_tail(acc[p], r0, acc[p][r0:] + pv[p]) for p in pairs]

    run = [(jnp.zeros((bq, g), F32), jnp.zeros((bq, g), F32)) for _ in pairs]
    acc = [jnp.zeros((bq, LANES), F32) for _ in pairs]
    base = i * bq
    for d in range(per):
        off = (per - 1 - d) * g
        mask = ((col + off) < row)[off:]
        run, acc = group(pl.multiple_of(base + off, g), run, acc, jnp.concatenate([mask, mask], axis=1), off)

    step = SB_STEP_GROUPS * g
    steps = i * (bq // step)

    def lowest(run):
        return jnp.min(functools.reduce(jnp.minimum, [r for pair in run for r in pair]))

    def more(carry):
        n, low, _, _ = carry
        return jnp.logical_and(n < steps, low < SB_DEAD_LOG2)

    def walk(carry):
        n, _, run, acc = carry
        start = base - (n + 1) * step
        for d in range(SB_STEP_GROUPS):
            run, acc = group(pl.multiple_of(start + (SB_STEP_GROUPS - 1 - d) * g, g), run, acc, None, 0)
        return n + 1, lowest(run), run, acc

    _, _, run, acc = lax.while_loop(more, walk, (jnp.int32(0), lowest(run), run, acc))
    for p in pairs:
        o_ref[:, pl_[p]] = acc[p].astype(o_ref.dtype)


def _sb_attn(sb, batch):
    t = sb.shape[0]
    s = t // batch
    bq = SB_QUERY_BLOCK
    nq = s // bq
    n = SB_HEADS * SB_DIM
    return pl.pallas_call(
        _sb_kernel,
        grid=(batch, nq),
        in_specs=[pl.BlockSpec((bq, n), lambda b, i: (b * nq + i, 0)),
                  pl.BlockSpec((s, n), lambda b, i: (b, 1)),
                  pl.BlockSpec((s, n), lambda b, i: (b, 2))],
        out_specs=pl.BlockSpec((bq, n), lambda b, i: (b * nq + i, 0)),
        out_shape=jax.ShapeDtypeStruct((t, n), BF16),
        compiler_params=_params(2),
        name="sb_attn",
    )(sb, sb, sb)


def _mla_kernel(q_ref, k_ref, v_ref, o_ref):
    bq = q_ref.shape[0]
    bk = MLA_KEY_BLOCK
    per = bq // bk
    i = pl.program_id(2)
    lane = lax.broadcasted_iota(jnp.int32, (1, LANES), 1)
    lo = lane < HALF
    q = q_ref[...]
    row = lax.broadcasted_iota(jnp.int32, (bq, bk), 0)
    col = lax.broadcasted_iota(jnp.int32, (bq, bk), 1)

    def visit(start, carry, mask, r0):
        stats, acc = carry
        k_b = k_ref[pl.ds(start, bk), :]
        v_b = v_ref[pl.ds(start, bk), :]
        new, probs, alphas = [], [], []
        for hh in range(2):
            m, l = stats[hh]
            sl = slice(hh * LANES, (hh + 1) * LANES)
            s = _dot_nt(q[r0:, sl], k_b[:, sl])
            if mask is not None:
                s = jnp.where(mask, s, -jnp.inf)
            m_new = jnp.maximum(m[r0:], jnp.max(s, axis=-1, keepdims=True))
            alpha = jnp.exp2(m[r0:] - m_new)
            p = jnp.exp2(s - m_new)
            new.append((_replace_tail(m, r0, m_new),
                        _replace_tail(l, r0, alpha * l[r0:] + jnp.sum(p, axis=-1, keepdims=True))))
            probs.append(p.astype(BF16))
            alphas.append(alpha)
        pv = _dot(jnp.concatenate(probs, axis=1), _head_rows(v_b, lo))
        return tuple(new), _replace_tail(acc, r0, jnp.where(lo, alphas[0], alphas[1]) * acc[r0:] + pv)

    carry = (tuple((jnp.full((bq, 1), -jnp.inf, F32), jnp.zeros((bq, 1), F32)) for _ in range(2)),
             jnp.zeros((bq, LANES), F32))
    base = i * bq
    for d in range(per):
        mask = ((col + d * bk) <= row)[d * bk:]
        carry = visit(pl.multiple_of(base + d * bk, bk), carry, mask, d * bk)

    def block(n, cr):
        for d in range(per):
            cr = visit(pl.multiple_of(n * bq + d * bk, bk), cr, None, 0)
        return cr

    stats, acc = lax.fori_loop(0, i, block, carry)
    o_ref[...] = (acc / jnp.where(lo, stats[0][1], stats[1][1])).astype(o_ref.dtype)


def _mla_attn(q, k, v, batch):
    t = q.shape[0]
    s = t // batch
    bq = MLA_BLOCK
    nq = s // bq
    pairs = MLA_HEADS // 2
    return pl.pallas_call(
        _mla_kernel,
        grid=(batch, pairs, nq),
        in_specs=[pl.BlockSpec((bq, 2 * LANES), lambda b, p, i: (b * nq + i, p)),
                  pl.BlockSpec((s, 2 * LANES), lambda b, p, i: (b, p)),
                  pl.BlockSpec((s, LANES), lambda b, p, i: (b, p))],
        out_specs=pl.BlockSpec((bq, LANES), lambda b, p, i: (b * nq + i, p)),
        out_shape=jax.ShapeDtypeStruct((t, MLA_HEADS * MLA_V), BF16),
        compiler_params=_params(3),
        name="mla_attn",
    )(q, k, v)


def _post_kernel(h_ref, og_ref, os_ref, om_ref, w_o_ref, g_ref, b_ref, w_in_ref, w_out_ref,
                 p_ref, w_g_ref, w_p_ref, out_ref):
    n_g = GDN_HEADS * GDN_DV
    n_s = SB_HEADS * SB_DIM

    def rows_of(r):
        x = h_ref[r, :]
        mix = (_dot(og_ref[r, :], w_o_ref[:n_g, :]) + _dot(os_ref[r, :], w_o_ref[n_g:n_g + n_s, :])
               + _dot(om_ref[r, :], w_o_ref[n_g + n_s:, :]))
        yield
        h1 = _layer_norm(DEEPNORM_ALPHA * x + mix, g_ref[1:2, :], b_ref[1:2, :])
        gu = _dot(h1.astype(BF16), w_in_ref[...])
        yield
        y = _dot((_silu(gu[:, :D_FF]) * gu[:, D_FF:]).astype(BF16), w_out_ref[...])
        yield
        h2 = _layer_norm(DEEPNORM_ALPHA * h1 + 0.5 * y, g_ref[2:3, :], b_ref[2:3, :])
        gate = _dot(h2.astype(BF16), w_g_ref[...])
        emb = _dot(p_ref[r, :].astype(BF16), w_p_ref[...])
        yield
        out_ref[r, :] = h2 + _sigmoid(gate) * emb

    _round_robin([rows_of(r) for r in _sub_tiles(h_ref.shape[0])])


def _post(layer, h, o_gdn, o_sb, o_mla, w_o, ln_g, ln_b, w_in, w_out, p, w_g, w_p):
    t = h.shape[0]
    tm = POST_TILE
    row = lambda n: pl.BlockSpec((tm, n), lambda i: (i, 0))
    return pl.pallas_call(
        _post_kernel,
        grid=(t // tm,),
        in_specs=[row(D_MODEL), row(o_gdn.shape[1]), row(o_sb.shape[1]), row(o_mla.shape[1]),
                  _of_layer(w_o, layer), _of_layer(ln_g, layer), _of_layer(ln_b, layer),
                  _of_layer(w_in, layer), _of_layer(w_out, layer),
                  pl.BlockSpec((None, tm, PLE_DIM), lambda i: (layer, i, 0)),
                  _of_layer(w_g, layer), _of_layer(w_p, layer)],
        out_specs=row(D_MODEL),
        out_shape=jax.ShapeDtypeStruct((t, D_MODEL), F32),
        compiler_params=_params(1),
        name="post",
    )(h, o_gdn, o_sb, o_mla, w_o, ln_g, ln_b, w_in, w_out, p, w_g, w_p)


def _zeros_like_cols(w, n):
    return jnp.zeros(w.shape[:-1] + (n,), w.dtype)


def _regroup_mix(w):
    edges = [N_QKV + N_Z, GDN_HEADS, GDN_HEADS, N_SB + MLA_Q_RANK + MLA_KV_RANK, MLA_ROPE]
    parts, o = [], 0
    for n in edges:
        parts.append(w[..., o:o + n])
        o += n
    gdn, ga, gb, att, kr = parts
    half = MLA_ROPE // 2
    pad = LANES - MLA_NOPE - MLA_ROPE
    m1 = [ga, gb, _zeros_like_cols(w, MLA_NOPE - 2 * GDN_HEADS), kr, _zeros_like_cols(w, pad)]
    m2 = [_zeros_like_cols(w, MLA_NOPE), kr[..., half:], kr[..., :half], _zeros_like_cols(w, pad)]
    return gdn.astype(BF16), att.astype(BF16), jnp.concatenate(m1 + m2, axis=-1).astype(BF16)


def _regroup_uq(w):
    d = MLA_NOPE + MLA_ROPE
    half = MLA_ROPE // 2
    pad = _zeros_like_cols(w, LANES - d)
    main, swap = [], []
    for h in range(MLA_HEADS):
        main += [w[..., h * d:(h + 1) * d], pad]
        swap += [_zeros_like_cols(w, MLA_NOPE), w[..., h * d + MLA_NOPE + half:(h + 1) * d],
                 w[..., h * d + MLA_NOPE:h * d + MLA_NOPE + half], pad]
    return jnp.concatenate(main + swap, axis=-1)


def _regroup_ukv(w):
    d = MLA_NOPE + MLA_V
    keys, vals = [], []
    for h in range(MLA_HEADS):
        keys += [w[..., h * d:h * d + MLA_NOPE], _zeros_like_cols(w, LANES - MLA_NOPE)]
        vals.append(w[..., h * d + MLA_NOPE:(h + 1) * d])
    return jnp.concatenate(keys + vals, axis=-1)


def _lane_rows(vals, start):
    return jnp.pad(vals.astype(F32), ((0, 0), (start, LANES - start - vals.shape[1])))


def _rope_rows():
    inv = 1.0 / (ROPE_BASE ** (jnp.arange(0, MLA_ROPE, 2, dtype=F32) / MLA_ROPE))
    half = MLA_ROPE // 2
    pad = LANES - MLA_NOPE - MLA_ROPE
    z = lambda n: jnp.zeros((n,), F32)
    one = lambda n: jnp.ones((n,), F32)
    freq = jnp.concatenate([z(MLA_NOPE), inv, inv, z(pad)])
    cos_mask = jnp.concatenate([one(MLA_NOPE + MLA_ROPE), z(pad)])
    sin_sign = jnp.concatenate([z(MLA_NOPE), -one(half), one(half), z(pad)])
    rows = jnp.stack([freq, cos_mask, sin_sign])
    return jnp.pad(rows, ((0, ROPE_ROWS - rows.shape[0]), (0, 0)))


def kernel(x, p, positions, ffa_w_in, ffa_w_out, mix_w_in, gdn_conv_w, gdn_a_log, gdn_dt_bias, gdn_norm_w, mla_q_norm_w, mla_kv_norm_w, mla_w_uq, mla_w_ukv, mix_w_o, ffb_w_in, ffb_w_out, ln_g, ln_b, ple_w_gate, ple_w_proj):
    batch, seq, d = x.shape
    t = batch * seq
    depth = p.shape[0]
    hd = GDN_HEADS * GDN_DK

    bf = lambda w: w.astype(BF16)
    w_mix, w_uq, w_ukv = _regroup_mix(mix_w_in), bf(_regroup_uq(mla_w_uq)), bf(_regroup_ukv(mla_w_ukv))
    ffa_in, ffa_out, ffb_in, ffb_out = bf(ffa_w_in), bf(ffa_w_out), bf(ffb_w_in), bf(ffb_w_out)
    w_o, w_g, w_p = bf(mix_w_o), bf(ple_w_gate), bf(ple_w_proj)
    alog, dtb = _lane_rows(gdn_a_log, A_LANE), _lane_rows(gdn_dt_bias, A_LANE)
    nw = jnp.concatenate([gdn_norm_w, gdn_norm_w], axis=-1).astype(F32)
    rope = _rope_rows()
    pos = positions.reshape(t, 1).astype(F32)
    p_rows = p.reshape(depth, t, p.shape[-1])

    h = x.reshape(t, d)
    for i in range(depth):
        h, qkv, z, sb, m1, q_m, k_m, v_m = _ffn_inproj(i, h, pos, rope, ffa_in, ffa_out, ln_g, ln_b, w_mix,
                                                       mla_q_norm_w, mla_kv_norm_w, w_uq, w_ukv)
        o_gdn = _gdn(i, qkv, z, m1, gdn_conv_w, alog, dtb, nw, batch)
        o_sb = _sb_attn(sb, batch)
        o_mla = _mla_attn(q_m, k_m, v_m, batch)
        h = _post(i, h, o_gdn, o_sb, o_mla, w_o, ln_g, ln_b, ffb_in, ffb_out, p_rows, w_g, w_p)
    return h.reshape(batch, seq, d)
```

```python
import functools

import jax
import jax.numpy as jnp
from jax import lax
from jax.experimental import pallas as pl
from jax.experimental.pallas import tpu as pltpu

F32 = jnp.float32
BF16 = jnp.bfloat16

DEPTH = 2
D_MODEL = 1024
PLE_DIM = 256
D_FF = 2816
LN_EPS = 1e-5
RMS_EPS = 1e-6
DEEPNORM_ALPHA = (2 * DEPTH) ** 0.25

GDN_HEADS = 8
GDN_DK = 64
GDN_DV = 64
GDN_CONV = 4
SB_HEADS = 4
SB_DIM = 64
MLA_HEADS = 4
MLA_NOPE = 64
MLA_ROPE = 32
MLA_V = 64
MLA_Q_RANK = 256
MLA_KV_RANK = 128
ROPE_BASE = 10000.0

LOG2E = 1.4426950408889634
LANES = 128
HALF = LANES // 2
VMEM_LIMIT = 56 * 1024 * 1024

GDN_CHUNK = 128
GDN_CHUNKS_PER_STEP = 4
GDN_GROUP = 2
SUB_TILE = 256
TOKEN_TILE = 512
POST_TILE = 512
SB_QUERY_BLOCK = 512
SB_STEP_GROUPS = 2
SB_DEAD_LOG2 = 150.0
MLA_BLOCK = 1024
MLA_KEY_BLOCK = 1024

N_HD = GDN_HEADS * GDN_DK
N_QKV = 3 * N_HD
N_Z = GDN_HEADS * GDN_DV
N_SB = 3 * SB_HEADS * SB_DIM
A_LANE = 0
B_LANE = GDN_HEADS
N_MLA_QK = MLA_HEADS * LANES
N_MLA_V = MLA_HEADS * MLA_V
SUBLANES = 8
ROPE_ROWS = SUBLANES
CONV_TAIL = SUBLANES


def _sigmoid(x):
    return 0.5 * jnp.tanh(0.5 * x) + 0.5


def _silu(x):
    half = 0.5 * x
    return half * jnp.tanh(half) + half


def _softplus(x):
    return jnp.maximum(x, 0.0) + jnp.log1p(jnp.exp(-jnp.abs(x)))


def _layer_norm(r, g, b):
    mu = jnp.mean(r, axis=-1, keepdims=True)
    d = r - mu
    var = jnp.mean(d * d, axis=-1, keepdims=True)
    return d * lax.rsqrt(var + LN_EPS) * g + b


def _rms_norm(x, w):
    return x * lax.rsqrt(jnp.mean(x * x, axis=-1, keepdims=True) + RMS_EPS) * w


def _dot(a, b):
    return jnp.dot(a, b, preferred_element_type=F32)


def _dot_nt(a, b):
    return lax.dot_general(a, b, (((1,), (1,)), ((), ())), preferred_element_type=F32)


def _head_rows(x, lo):
    zero = jnp.zeros_like(x)
    return jnp.concatenate([jnp.where(lo, x, zero), jnp.where(lo, zero, x)], axis=0)


def _replace_tail(full, r0, tail):
    return tail if r0 == 0 else jnp.concatenate([full[:r0], tail], axis=0)


def _split2(x):
    hi = x.astype(BF16)
    lo = (x - hi.astype(F32)).astype(BF16)
    return hi, lo


def _split3(x):
    hi = x.astype(BF16)
    r = x - hi.astype(F32)
    mid = r.astype(BF16)
    lo = (r - mid.astype(F32)).astype(BF16)
    return hi, mid, lo


def _swiglu(xb, w_in_ref, w_out_ref):
    gu = _dot(xb, w_in_ref[...])
    act = _silu(gu[:, :D_FF]) * gu[:, D_FF:]
    return _dot(act.astype(BF16), w_out_ref[...])


def _sub_tiles(rows):
    return [slice(s, s + SUB_TILE) for s in range(0, rows, SUB_TILE)]


def _round_robin(gens):
    live = list(gens)
    while live:
        for g in list(live):
            try:
                next(g)
            except StopIteration:
                live.remove(g)


def _whole(a):
    return pl.BlockSpec(a.shape, lambda *_: (0,) * a.ndim, pipeline_mode=pl.Buffered(1))


def _of_layer(a, layer, block=None, at=None):
    tail = tuple(a.shape[1:]) if block is None else tuple(block)
    idx = (0,) * len(tail) if at is None else tuple(at)
    return pl.BlockSpec((None,) + tail, lambda *_: (layer,) + idx, pipeline_mode=pl.Buffered(1))


def _params(n_axes):
    return pltpu.CompilerParams(dimension_semantics=("arbitrary",) * n_axes, vmem_limit_bytes=VMEM_LIMIT)


def _ffn_inproj_kernel(layer, h_ref, pos_ref, rope_ref, w_in_ref, w_out_ref, g_ref, b_ref,
                       w_gdn_ref, w_att_ref, w_m_ref, qnw_ref, kvnw_ref, w_uq_ref, w_ukv_ref,
                       h_out, qkv_out, z_out, sb_out, m1_out, mq_out, mk_out, mv_out):
    lane = lax.broadcasted_iota(jnp.int32, (1, LANES), 1)
    rot = (lane >= MLA_NOPE) & (lane < MLA_NOPE + MLA_ROPE)
    n_q = SB_HEADS * SB_DIM
    scale = (MLA_NOPE + MLA_ROPE) ** -0.5 * LOG2E

    def rows_of(r):
        x = h_ref[r, :]
        gu = _dot(x.astype(BF16), w_in_ref[...])
        yield
        y = _dot((_silu(gu[:, :D_FF]) * gu[:, D_FF:]).astype(BF16), w_out_ref[...])
        ang = pos_ref[r, :] * rope_ref[0:1, :]
        cos = jnp.cos(ang) * rope_ref[1:2, :]
        sin = jnp.sin(ang) * rope_ref[2:3, :]
        yield
        hn = _layer_norm(DEEPNORM_ALPHA * x + 0.5 * y, g_ref[0:1, :], b_ref[0:1, :])
        h_out[r, :] = hn
        hb = hn.astype(BF16)
        att = _dot(hb, w_att_ref[...])
        m12 = _dot(hb, w_m_ref[...])
        yield
        gdn = _dot(hb, w_gdn_ref[...])
        sb_out[r, :n_q] = (att[:, :n_q] * (SB_DIM ** -0.5 * LOG2E)).astype(BF16)
        sb_out[r, n_q:] = att[:, n_q:N_SB].astype(BF16)
        mq = att[:, N_SB:N_SB + MLA_Q_RANK]
        ckv = att[:, N_SB + MLA_Q_RANK:]
        m1 = m12[:, :LANES]
        m2 = m12[:, LANES:]
        m1_out[r, :] = m1
        qf = _dot(_rms_norm(mq, qnw_ref[layer:layer + 1, :]).astype(BF16), w_uq_ref[...])
        kv = _dot(_rms_norm(ckv, kvnw_ref[layer:layer + 1, :]).astype(BF16), w_ukv_ref[...])
        yield
        qkv_out[r, :] = gdn[:, :N_QKV]
        z_out[r, :] = gdn[:, N_QKV:]
        k_rot = jnp.where(rot, m1 * cos, 0.0) + m2 * sin
        for h in range(MLA_HEADS):
            sl = slice(h * LANES, (h + 1) * LANES)
            q_h = qf[:, sl] * cos + qf[:, N_MLA_QK + h * LANES:N_MLA_QK + (h + 1) * LANES] * sin
            mq_out[r, sl] = (q_h * scale).astype(BF16)
            mk_out[r, sl] = (kv[:, sl] + k_rot).astype(BF16)
        mv_out[r, :] = kv[:, N_MLA_QK:].astype(BF16)

    _round_robin([rows_of(r) for r in _sub_tiles(h_ref.shape[0])])


def _ffn_inproj(layer, h, pos, rope, w_in, w_out, ln_g, ln_b, w_mix, qnw, kvnw, w_uq, w_ukv):
    w_gdn, w_att, w_m = w_mix
    t = h.shape[0]
    tm = TOKEN_TILE
    row = lambda n: pl.BlockSpec((tm, n), lambda i: (i, 0))
    widths = (D_MODEL, N_QKV, N_Z, N_SB, LANES, N_MLA_QK, N_MLA_QK, N_MLA_V)
    dtypes = (F32, F32, F32, BF16, F32, BF16, BF16, BF16)
    return pl.pallas_call(
        functools.partial(_ffn_inproj_kernel, layer),
        grid=(t // tm,),
        in_specs=[row(D_MODEL), row(1), _whole(rope), _of_layer(w_in, layer), _of_layer(w_out, layer),
                  _of_layer(ln_g, layer), _of_layer(ln_b, layer),
                  _of_layer(w_gdn, layer), _of_layer(w_att, layer), _of_layer(w_m, layer),
                  _whole(qnw), _whole(kvnw), _of_layer(w_uq, layer), _of_layer(w_ukv, layer)],
        out_specs=[row(n) for n in widths],
        out_shape=[jax.ShapeDtypeStruct((t, n), dt) for n, dt in zip(widths, dtypes)],
        compiler_params=_params(1),
        name="ffn_inproj",
    )(h, pos, rope, w_in, w_out, ln_g, ln_b, w_gdn, w_att, w_m, qnw, kvnw, w_uq, w_ukv)


def _unit_lower_inverses(l_strict, row, col):
    c = l_strict[0].shape[0]
    blk = lambda v, log2: jnp.right_shift(v, log2)
    pair = (blk(row, 1) == blk(col, 1)) & (row > col)
    eye = jnp.where(row == col, 1.0, 0.0)
    xs = [eye - jnp.where(pair, l, 0.0) for l in l_strict]
    log2 = 1
    while (2 << log2) <= c:
        off = (blk(row, log2 + 1) == blk(col, log2 + 1)) & (blk(row, log2) > blk(col, log2))
        xb = [x.astype(BF16) for x in xs]
        lx = [_dot(jnp.where(off, l, 0.0).astype(BF16), b).astype(BF16) for l, b in zip(l_strict, xb)]
        yield
        xs = [x - _dot(b, t) for x, b, t in zip(xs, xb, lx)]
        yield
        log2 += 1
    return xs


def _interleave(main, side):
    values, live = [None, None], [main, side]
    while any(g is not None for g in live):
        for n, g in enumerate(live):
            if g is not None:
                try:
                    next(g)
                except StopIteration as stop:
                    values[n], live[n] = stop.value, None
    return values


def _dot_exact(a, b, terms):
    split = {2: _split2, 3: _split3}[terms]
    if a.dtype == F32:
        parts = [_dot(piece, b) for piece in split(a)]
    else:
        parts = [_dot(a, piece) for piece in split(b)]
    return functools.reduce(lambda x, y: x + y, parts)


def _gdn_kernel(layer, qkv_ref, z_ref, m1_ref, cw_ref, alog_ref, dtb_ref, nw_ref, o_ref, buf, state_ref):
    c = GDN_CHUNK
    chunks = qkv_ref.shape[0] // c
    pairs = GDN_HEADS // 2
    first = pl.program_id(1) == 0

    @pl.when(first)
    def _():
        state_ref[...] = jnp.zeros(state_ref.shape, F32)

    lane = lax.broadcasted_iota(jnp.int32, (1, LANES), 1)
    lo = lane < HALF
    row = lax.broadcasted_iota(jnp.int32, (c, c), 0)
    col = lax.broadcasted_iota(jnp.int32, (c, c), 1)
    incl = row >= col
    strict = row > col
    tril = jnp.where(incl, 1.0, 0.0).astype(BF16)
    half_log2 = HALF.bit_length() - 1
    same_head = jnp.where(jnp.right_shift(row, half_log2) == jnp.right_shift(col, half_log2), 1.0, 0.0).astype(BF16)

    def _head_sums(sq):
        return _dot(sq.astype(BF16), same_head)

    a_row = -jnp.exp(alog_ref[layer:layer + 1, :])
    dt_row = dtb_ref[layer:layer + 1, :]
    nw = nw_ref[layer:layer + 1, :]
    heads = range(GDN_HEADS)
    ps =[slice(p * LANES, (p + 1) * LANES) for p in range(pairs)]

    @pl.when(first)
    def _():
        buf[...] = jnp.zeros(buf.shape, F32)


    def prepare(j):
        rows = slice(j * c, (j + 1) * c)

        def conv_silu(group):
            sl = slice(group * LANES, (group + 1) * LANES)
            x = qkv_ref[rows, sl]
            before = buf[:, sl] if j == 0 else qkv_ref[j * c - CONV_TAIL:j * c, sl]
            xx = jnp.concatenate([before, x], axis=0)
            y = x * cw_ref[GDN_CONV - 1:GDN_CONV, sl]
            for tap in range(GDN_CONV - 1):
                y = y + pltpu.roll(xx, GDN_CONV - 1 - tap, axis=0)[CONV_TAIL:] * cw_ref[tap:tap + 1, sl]
            return _silu(y)

        m1 = m1_ref[rows, :]
        gc = _dot_exact(tril, a_row * _softplus(m1 + dt_row), 3)
        gc_rows = gc.T
        beta_all = _sigmoid(m1)
        yield
        out = []
        for p in range(pairs):
            xq, xk, xv = conv_silu(p), conv_silu(pairs + p), conv_silu(2 * pairs + p)
            yield
            qn = xq * (lax.rsqrt(_head_sums(xq * xq) + RMS_EPS) * (GDN_DK ** -0.5))
            kn = xk * lax.rsqrt(_head_sums(xk * xk) + RMS_EPS)
            yield
            for h in (2 * p, 2 * p + 1):
                mine = lo if h % 2 == 0 else jnp.logical_not(lo)
                gcb = jnp.broadcast_to(gc[:, A_LANE + h:A_LANE + h + 1], (c, LANES))
                gc_row = gc_rows[A_LANE + h:A_LANE + h + 1, :]
                out.append(dict(
                    q=jnp.where(mine, qn, 0.0), k=jnp.where(mine, kn, 0.0), v=jnp.where(mine, xv, 0.0),
                    gcb=gcb, beta=jnp.broadcast_to(beta_all[:, B_LANE + h:B_LANE + h + 1], (c, LANES)),
                    decay=jnp.where(incl, jnp.exp(jnp.where(incl, gcb - gc_row, 0.0)), 0.0), e=jnp.exp(gcb)))
                yield
        return out

    def solve(pre):
        gram = [_dot_nt(jnp.concatenate([t["q"], t["k"]], axis=0).astype(BF16), t["k"].astype(BF16))
                for t in pre]
        yield
        l_strict = [jnp.where(strict, g[c:] * t["decay"] * t["beta"], 0.0) for g, t in zip(gram, pre)]
        t_inv = yield from _unit_lower_inverses(l_strict, row, col)
        out = []
        for g, t, ti in zip(gram, pre, t_inv):
            uw = _dot(ti.astype(BF16), jnp.concatenate([t["v"] * t["beta"], t["k"] * (t["beta"] * t["e"])],
                                                       axis=1).astype(BF16))
            g_end = t["gcb"][c - 1:c, :]
            out.append(dict(
                u=uw[:, :LANES], wq=jnp.concatenate([uw[:, LANES:], t["q"] * t["e"]], axis=0).astype(BF16),
                qk_kd=jnp.concatenate([g[:c] * t["decay"], (t["k"] * jnp.exp(g_end - t["gcb"])).T],
                                      axis=0).astype(BF16),
                keep=jnp.exp(g_end)))
        yield
        return out

    def advance(j, sol, state):
        ws = [_dot(s["wq"], st.astype(BF16)) for s, st in zip(sol, state)]
        yield
        fin = [_dot(s["qk_kd"], (s["u"] - w[:c]).astype(BF16)) for s, w in zip(sol, ws)]
        yield
        new_state = [st * s["keep"] + f[c:] for st, s, f in zip(state, sol, fin)]
        rows = slice(j * c, (j + 1) * c)
        for p in range(pairs):
            o_pair = (ws[2 * p][c:] + fin[2 * p][:c]) + (ws[2 * p + 1][c:] + fin[2 * p + 1][:c])
            ms = _head_sums(o_pair * o_pair) * (1.0 / GDN_DV)
            o_ref[rows, ps[p]] = (o_pair * lax.rsqrt(ms + RMS_EPS) * nw * _silu(z_ref[rows, ps[p]])).astype(o_ref.dtype)
            yield
        return new_state

    def prepare_group(js):
        out = []
        for j in js:
            out += yield from prepare(j)
        return out

    def group_matmuls(js, pre, state):
        sol = yield from solve(pre)
        for n, j in enumerate(js):
            state = yield from advance(j, sol[n * GDN_HEADS:(n + 1) * GDN_HEADS], state)
        return state

    def nothing():
        return None
        yield

    groups = [list(range(g0, min(g0 + GDN_GROUP, chunks))) for g0 in range(0, chunks, GDN_GROUP)]
    state = [state_ref[h] for h in heads]
    _, pre = _interleave(nothing(), prepare_group(groups[0]))
    for n, js in enumerate(groups):
        upcoming = prepare_group(groups[n + 1]) if n + 1 < len(groups) else nothing()
        state, pre = _interleave(group_matmuls(js, pre, state), upcoming)
    for h in heads:
        state_ref[h] = state[h]
    buf[...] = qkv_ref[chunks * c - CONV_TAIL:chunks * c, :]


def _gdn(layer, qkv, z, m1, conv_w, alog, dtb, nw, batch):
    t = qkv.shape[0]
    c = GDN_CHUNK * GDN_CHUNKS_PER_STEP
    nc = t // batch // c
    blk = lambda n: pl.BlockSpec((c, n), lambda b, i: (b * nc + i, 0))
    return pl.pallas_call(
        functools.partial(_gdn_kernel, layer),
        grid=(batch, nc),
        in_specs=[blk(N_QKV), blk(N_Z), blk(LANES), _of_layer(conv_w, layer),
                  _whole(alog), _whole(dtb), _whole(nw)],
        out_specs=blk(GDN_HEADS * GDN_DV),
        out_shape=jax.ShapeDtypeStruct((t, GDN_HEADS * GDN_DV), BF16),
        scratch_shapes=[pltpu.VMEM((CONV_TAIL, N_QKV), F32), pltpu.VMEM((GDN_HEADS, LANES, LANES), F32)],
        compiler_params=_params(2),
        name="gdn",
    )(qkv, z, m1, conv_w, alog, dtb, nw)


def _sb_kernel(q_ref, k_ref, v_ref, o_ref):
    bq = q_ref.shape[0]
    g = LANES
    per = bq // g
    pairs = range(q_ref.shape[1] // LANES)
    pl_ = [slice(p * LANES, (p + 1) * LANES) for p in pairs]
    i = pl.program_id(1)
    lane = lax.broadcasted_iota(jnp.int32, (1, LANES), 1)
    lo = lane < HALF
    q = q_ref[...]
    row = lax.broadcasted_iota(jnp.int32, (bq, g), 0)
    col = lax.broadcasted_iota(jnp.int32, (bq, g), 1)
    jr = lax.broadcasted_iota(jnp.int32, (g, g), 0)
    jc = lax.broadcasted_iota(jnp.int32, (g, g), 1)
    later = jnp.where(jr > jc, 1.0, 0.0).astype(BF16)
    suffix_total = jnp.concatenate([later, jnp.ones((g, g), BF16)], axis=1)
    suffix_total = jnp.concatenate([suffix_total, suffix_total], axis=0)

    def group(start, run, acc, mask, r0):
        k_g = k_ref[pl.ds(start, g), :]
        v_g = v_ref[pl.ds(start, g), :]
        halves = [slice(hh * g, (hh + 1) * g) for hh in range(2)]
        z = [_dot_nt(q[r0:, pl_[p]], _head_rows(k_g[:, pl_[p]], lo)) for p in pairs]
        nl = [jnp.maximum(zp, 0.0) + jnp.log(1.0 + jnp.exp2(-jnp.abs(zp))) * LOG2E for zp in z]
        if mask is not None:
            nl = [jnp.where(mask, n, 0.0) for n in nl]
        st = [[_dot(jnp.concatenate(_split2(nl[p][:, sl]), axis=1), suffix_total) for sl in halves]
              for p in pairs]
        wts = [jnp.concatenate([jnp.exp2(z[p][:, sl] - nl[p][:, sl] - (run[p][hh][r0:] + st[p][hh][:, :g]))
                                for hh, sl in enumerate(halves)], axis=1) for p in pairs]
        if mask is not None:
            wts = [jnp.where(mask, w, 0.0) for w in wts]
        new_run = [tuple(_replace_tail(run[p][hh], r0, run[p][hh][r0:] + st[p][hh][:, g:]) for hh in range(2))
                   for p in pairs]
        pv = [_dot(wts[p].astype(BF16), _head_rows(v_g[:, pl_[p]], lo)) for p in pairs]
        return new_run, [_replace_tail(acc[p], r0, acc[p][r0:] + pv[p]) for p in pairs]

    run = [(jnp.zeros((bq, g), F32), jnp.zeros((bq, g), F32)) for _ in pairs]
    acc = [jnp.zeros((bq, LANES), F32) for _ in pairs]
    base = i * bq
    for d in range(per):
        off = (per - 1 - d) * g
        mask = ((col + off) < row)[off:]
        run, acc = group(pl.multiple_of(base + off, g), run, acc, jnp.concatenate([mask, mask], axis=1), off)

    step = SB_STEP_GROUPS * g
    steps = i * (bq // step)

    def lowest(run):
        return jnp.min(functools.reduce(jnp.minimum, [r for pair in run for r in pair]))

    def more(carry):
        n, low, _, _ = carry
        return jnp.logical_and(n < steps, low < SB_DEAD_LOG2)

    def walk(carry):
        n, _, run, acc = carry
        start = base - (n + 1) * step
        for d in range(SB_STEP_GROUPS):
            run, acc = group(pl.multiple_of(start + (SB_STEP_GROUPS - 1 - d) * g, g), run, acc, None, 0)
        return n + 1, lowest(run), run, acc

    _, _, run, acc = lax.while_loop(more, walk, (jnp.int32(0), lowest(run), run, acc))
    for p in pairs:
        o_ref[:, pl_[p]] = acc[p].astype(o_ref.dtype)


def _sb_attn(sb, batch):
    t = sb.shape[0]
    s = t // batch
    bq = SB_QUERY_BLOCK
    nq = s // bq
    n = SB_HEADS * SB_DIM
    return pl.pallas_call(
        _sb_kernel,
        grid=(batch, nq),
        in_specs=[pl.BlockSpec((bq, n), lambda b, i: (b * nq + i, 0)),
                  pl.BlockSpec((s, n), lambda b, i: (b, 1)),
                  pl.BlockSpec((s, n), lambda b, i: (b, 2))],
        out_specs=pl.BlockSpec((bq, n), lambda b, i: (b * nq + i, 0)),
        out_shape=jax.ShapeDtypeStruct((t, n), BF16),
        compiler_params=_params(2),
        name="sb_attn",
    )(sb, sb, sb)


def _mla_kernel(q_ref, k_ref, v_ref, o_ref):
    bq = q_ref.shape[0]
    bk = MLA_KEY_BLOCK
    per = bq // bk
    i = pl.program_id(2)
    lane = lax.broadcasted_iota(jnp.int32, (1, LANES), 1)
    lo = lane < HALF
    q = q_ref[...]
    row = lax.broadcasted_iota(jnp.int32, (bq, bk), 0)
    col = lax.broadcasted_iota(jnp.int32, (bq, bk), 1)

    def visit(start, carry, mask, r0):
        stats, acc = carry
        k_b = k_ref[pl.ds(start, bk), :]
        v_b = v_ref[pl.ds(start, bk), :]
        new, probs, alphas = [], [], []
        for hh in range(2):
            m, l = stats[hh]
            sl = slice(hh * LANES, (hh + 1) * LANES)
            s = _dot_nt(q[r0:, sl], k_b[:, sl])
            if mask is not None:
                s = jnp.where(mask, s, -jnp.inf)
            m_new = jnp.maximum(m[r0:], jnp.max(s, axis=-1, keepdims=True))
            alpha = jnp.exp2(m[r0:] - m_new)
            p = jnp.exp2(s - m_new)
            new.append((_replace_tail(m, r0, m_new),
                        _replace_tail(l, r0, alpha * l[r0:] + jnp.sum(p, axis=-1, keepdims=True))))
            probs.append(p.astype(BF16))
            alphas.append(alpha)
        pv = _dot(jnp.concatenate(probs, axis=1), _head_rows(v_b, lo))
        return tuple(new), _replace_tail(acc, r0, jnp.where(lo, alphas[0], alphas[1]) * acc[r0:] + pv)

    carry = (tuple((jnp.full((bq, 1), -jnp.inf, F32), jnp.zeros((bq, 1), F32)) for _ in range(2)),
             jnp.zeros((bq, LANES), F32))
    base = i * bq
    for d in range(per):
        mask = ((col + d * bk) <= row)[d * bk:]
        carry = visit(pl.multiple_of(base + d * bk, bk), carry, mask, d * bk)

    def block(n, cr):
        for d in range(per):
            cr = visit(pl.multiple_of(n * bq + d * bk, bk), cr, None, 0)
        return cr

    stats, acc = lax.fori_loop(0, i, block, carry)
    o_ref[...] = (acc / jnp.where(lo, stats[0][1], stats[1][1])).astype(o_ref.dtype)


def _mla_attn(q, k, v, batch):
    t = q.shape[0]
    s = t // batch
    bq = MLA_BLOCK
    nq = s // bq
    pairs = MLA_HEADS // 2
    return pl.pallas_call(
        _mla_kernel,
        grid=(batch, pairs, nq),
        in_specs=[pl.BlockSpec((bq, 2 * LANES), lambda b, p, i: (b * nq + i, p)),
                  pl.BlockSpec((s, 2 * LANES), lambda b, p, i: (b, p)),
                  pl.BlockSpec((s, LANES), lambda b, p, i: (b, p))],
        out_specs=pl.BlockSpec((bq, LANES), lambda b, p, i: (b * nq + i, p)),
        out_shape=jax.ShapeDtypeStruct((t, MLA_HEADS * MLA_V), BF16),
        compiler_params=_params(3),
        name="mla_attn",
    )(q, k, v)


def _post_kernel(h_ref, og_ref, os_ref, om_ref, w_o_ref, g_ref, b_ref, w_in_ref, w_out_ref,
                 p_ref, w_g_ref, w_p_ref, out_ref):
    n_g = GDN_HEADS * GDN_DV
    n_s = SB_HEADS * SB_DIM

    def rows_of(r):
        x = h_ref[r, :]
        mix = (_dot(og_ref[r, :], w_o_ref[:n_g, :]) + _dot(os_ref[r, :], w_o_ref[n_g:n_g + n_s, :])
               + _dot(om_ref[r, :], w_o_ref[n_g + n_s:, :]))
        yield
        h1 = _layer_norm(DEEPNORM_ALPHA * x + mix, g_ref[1:2, :], b_ref[1:2, :])
        gu = _dot(h1.astype(BF16), w_in_ref[...])
        yield
        y = _dot((_silu(gu[:, :D_FF]) * gu[:, D_FF:]).astype(BF16), w_out_ref[...])
        yield
        h2 = _layer_norm(DEEPNORM_ALPHA * h1 + 0.5 * y, g_ref[2:3, :], b_ref[2:3, :])
        gate = _dot(h2.astype(BF16), w_g_ref[...])
        emb = _dot(p_ref[r, :].astype(BF16), w_p_ref[...])
        yield
        out_ref[r, :] = h2 + _sigmoid(gate) * emb

    _round_robin([rows_of(r) for r in _sub_tiles(h_ref.shape[0])])


def _post(layer, h, o_gdn, o_sb, o_mla, w_o, ln_g, ln_b, w_in, w_out, p, w_g, w_p):
    t = h.shape[0]
    tm = POST_TILE
    row = lambda n: pl.BlockSpec((tm, n), lambda i: (i, 0))
    return pl.pallas_call(
        _post_kernel,
        grid=(t // tm,),
        in_specs=[row(D_MODEL), row(o_gdn.shape[1]), row(o_sb.shape[1]), row(o_mla.shape[1]),
                  _of_layer(w_o, layer), _of_layer(ln_g, layer), _of_layer(ln_b, layer),
                  _of_layer(w_in, layer), _of_layer(w_out, layer),
                  pl.BlockSpec((None, tm, PLE_DIM), lambda i: (layer, i, 0)),
                  _of_layer(w_g, layer), _of_layer(w_p, layer)],
        out_specs=row(D_MODEL),
        out_shape=jax.ShapeDtypeStruct((t, D_MODEL), F32),
        compiler_params=_params(1),
        name="post",
    )(h, o_gdn, o_sb, o_mla, w_o, ln_g, ln_b, w_in, w_out, p, w_g, w_p)


def _zeros_like_cols(w, n):
    return jnp.zeros(w.shape[:-1] + (n,), w.dtype)


def _regroup_mix(w):
    edges = [N_QKV + N_Z, GDN_HEADS, GDN_HEADS, N_SB + MLA_Q_RANK + MLA_KV_RANK, MLA_ROPE]
    parts, o = [], 0
    for n in edges:
        parts.append(w[..., o:o + n])
        o += n
    gdn, ga, gb, att, kr = parts
    half = MLA_ROPE // 2
    pad = LANES - MLA_NOPE - MLA_ROPE
    m1 = [ga, gb, _zeros_like_cols(w, MLA_NOPE - 2 * GDN_HEADS), kr, _zeros_like_cols(w, pad)]
    m2 = [_zeros_like_cols(w, MLA_NOPE), kr[..., half:], kr[..., :half], _zeros_like_cols(w, pad)]
    return gdn.astype(BF16), att.astype(BF16), jnp.concatenate(m1 + m2, axis=-1).astype(BF16)


def _regroup_uq(w):
    d = MLA_NOPE + MLA_ROPE
    half = MLA_ROPE // 2
    pad = _zeros_like_cols(w, LANES - d)
    main, swap = [], []
    for h in range(MLA_HEADS):
        main += [w[..., h * d:(h + 1) * d], pad]
        swap += [_zeros_like_cols(w, MLA_NOPE), w[..., h * d + MLA_NOPE + half:(h + 1) * d],
                 w[..., h * d + MLA_NOPE:h * d + MLA_NOPE + half], pad]
    return jnp.concatenate(main + swap, axis=-1)


def _regroup_ukv(w):
    d = MLA_NOPE + MLA_V
    keys, vals = [], []
    for h in range(MLA_HEADS):
        keys += [w[..., h * d:h * d + MLA_NOPE], _zeros_like_cols(w, LANES - MLA_NOPE)]
        vals.append(w[..., h * d + MLA_NOPE:(h + 1) * d])
    return jnp.concatenate(keys + vals, axis=-1)


def _lane_rows(vals, start):
    return jnp.pad(vals.astype(F32), ((0, 0), (start, LANES - start - vals.shape[1])))


def _rope_rows():
    inv = 1.0 / (ROPE_BASE ** (jnp.arange(0, MLA_ROPE, 2, dtype=F32) / MLA_ROPE))
    half = MLA_ROPE // 2
    pad = LANES - MLA_NOPE - MLA_ROPE
    z = lambda n: jnp.zeros((n,), F32)
    one = lambda n: jnp.ones((n,), F32)
    freq = jnp.concatenate([z(MLA_NOPE), inv, inv, z(pad)])
    cos_mask = jnp.concatenate([one(MLA_NOPE + MLA_ROPE), z(pad)])
    sin_sign = jnp.concatenate([z(MLA_NOPE), -one(half), one(half), z(pad)])
    rows = jnp.stack([freq, cos_mask, sin_sign])
    return jnp.pad(rows, ((0, ROPE_ROWS - rows.shape[0]), (0, 0)))


def kernel(x, p, positions, ffa_w_in, ffa_w_out, mix_w_in, gdn_conv_w, gdn_a_log, gdn_dt_bias, gdn_norm_w, mla_q_norm_w, mla_kv_norm_w, mla_w_uq, mla_w_ukv, mix_w_o, ffb_w_in, ffb_w_out, ln_g, ln_b, ple_w_gate, ple_w_proj):
    batch, seq, d = x.shape
    t = batch * seq
    depth = p.shape[0]

    bf = lambda w: w.astype(BF16)
    w_mix, w_uq, w_ukv = _regroup_mix(mix_w_in), bf(_regroup_uq(mla_w_uq)), bf(_regroup_ukv(mla_w_ukv))
    ffa_in, ffa_out, ffb_in, ffb_out = bf(ffa_w_in), bf(ffa_w_out), bf(ffb_w_in), bf(ffb_w_out)
    w_o, w_g, w_p = bf(mix_w_o), bf(ple_w_gate), bf(ple_w_proj)
    alog, dtb = _lane_rows(gdn_a_log, A_LANE), _lane_rows(gdn_dt_bias, A_LANE)
    nw = jnp.concatenate([gdn_norm_w, gdn_norm_w], axis=-1).astype(F32)
    rope = _rope_rows()
    pos = positions.reshape(t, 1).astype(F32)
    p_rows = p.reshape(depth, t, p.shape[-1])

    h = x.reshape(t, d)
    for i in range(depth):
        h, qkv, z, sb, m1, q_m, k_m, v_m = _ffn_inproj(i, h, pos, rope, ffa_in, ffa_out, ln_g, ln_b, w_mix,
                                                       mla_q_norm_w, mla_kv_norm_w, w_uq, w_ukv)
        o_gdn = _gdn(i, qkv, z, m1, gdn_conv_w, alog, dtb, nw, batch)
        o_sb = _sb_attn(sb, batch)
        o_mla = _mla_attn(q_m, k_m, v_m, batch)
        h = _post(i, h, o_gdn, o_sb, o_mla, w_o, ln_g, ln_b, ffb_in, ffb_out, p_rows, w_g, w_p)
    return h.reshape(batch, seq, d)
```
